```python
import math
import jax, jax.numpy as jnp
from jax import lax
import numpy as np

D_MODEL = 1024
BATCH = 4
SEQ = 8192
DEPTH = 2
DEC_BATCH = 8
DEC_SEQ = 4096
PAST_LEN = 128

N_AB = (DEPTH + 1) // 2
N_C = DEPTH // 2

CONV_WIDTH = D_MODEL // 2
CONV_KERNEL = 31
HYENA_WIDTH = D_MODEL // 2
HYENA_ORDER = 2
HYENA_IN = (HYENA_ORDER + 1) * HYENA_WIDTH
SHORT_KERNEL = 3
HYENA_EMB_DIM = 33
HYENA_FILTER_HIDDEN = 64
HYENA_DECAY_TARGET = 1e-2
HYENA_FAST_DECAY_PCT = 0.3
HYENA_SLOW_DECAY_PCT = 1.5
N_FILTER_CH = 2 * HYENA_ORDER * HYENA_WIDTH
AB_IN_WIDTH = 2 * CONV_WIDTH + HYENA_IN
N_HEADS = 16
N_KV_HEADS = 4
HEAD_DIM = 64
GROUP = N_HEADS // N_KV_HEADS
ROT_DIM = HEAD_DIM // 4
ROPE_THETA = 500000.0
WINDOW = 128
BLOCK = 128
QKV_WIDTH = (N_HEADS + 2 * N_KV_HEADS) * HEAD_DIM
D_FF = 4 * D_MODEL
NORM_EPS = 1e-5
LN_EPS = 1e-5
FILTER_EPS = 1e-6

kernel_name = 'hybrid_conformer_hyena_swa_encoder'


def rms_norm(x, g):
    xf = x.astype(jnp.float32)
    y = xf * lax.rsqrt(jnp.mean(xf * xf, axis=-1, keepdims=True) + NORM_EPS)
    return (y * g.astype(jnp.float32)).astype(x.dtype)


def layer_norm(x, g, b):
    xf = x.astype(jnp.float32)
    mu = jnp.mean(xf, axis=-1, keepdims=True)
    var = jnp.mean(jnp.square(xf - mu), axis=-1, keepdims=True)
    y = (xf - mu) * lax.rsqrt(var + LN_EPS)
    return (y * g.astype(jnp.float32) + b.astype(jnp.float32)).astype(x.dtype)


def depthwise_conv(x, w, b):
    k, c = w.shape
    pad = k // 2
    y = lax.conv_general_dilated(
        x, w[:, None, :].astype(x.dtype), window_strides=(1,), padding=[(pad, pad)],
        dimension_numbers=('NWC', 'WIO', 'NWC'), feature_group_count=c)
    return y + b.astype(x.dtype)


def conformer_conv(u, w_dw, b_dw, ln_g, ln_b):
    a, gate = jnp.split(u, 2, axis=-1)
    h = a * jax.nn.sigmoid(gate)
    h = depthwise_conv(h, w_dw, b_dw)
    h = layer_norm(h, ln_g, ln_b)
    return jax.nn.silu(h)


def hyena_filters(L, w1, b1, w2, b2, w3, b3, w4, freq, decay):
    f32 = jnp.float32
    t = jnp.linspace(0.0, 1.0, L, dtype=f32)[:, None]
    bands = (HYENA_EMB_DIM - 1) // 2
    w = 2.0 * math.pi * jnp.arange(L, dtype=f32) / L
    f = jnp.linspace(1e-4, bands - 1, bands, dtype=f32)
    fw = w[:, None] * f[None, :]
    z = jnp.concatenate([t, jnp.cos(fw), -jnp.sin(fw)], axis=-1)
    fr = freq.astype(f32)
    h = jnp.sin(fr[0] * (z @ w1.astype(f32) + b1.astype(f32)))
    h = jnp.sin(fr[1] * (h @ w2.astype(f32) + b2.astype(f32)))
    h = jnp.sin(fr[2] * (h @ w3.astype(f32) + b3.astype(f32)))
    h = h @ w4.astype(f32)
    h = h * jnp.exp(-t * jnp.abs(decay.astype(f32)))
    h = h.reshape(L, HYENA_ORDER, 2, HYENA_WIDTH)
    fwd, bwd = h[:, :, 0], h[:, :, 1]
    zero = jnp.zeros((1, HYENA_ORDER, HYENA_WIDTH), f32)
    k = jnp.concatenate([fwd, zero, bwd[:0:-1]], axis=0)
    k = k * lax.rsqrt(jnp.sum(k * k, axis=0, keepdims=True) + FILTER_EPS)
    return k


def hyena(u, short_w, short_b, w1, b1, w2, b2, w3, b3, w4, freq, decay, skip):
    L = u.shape[1]
    u = depthwise_conv(u, short_w, short_b)
    x1, x2, v = jnp.split(u, 3, axis=-1)
    k = hyena_filters(L, w1, b1, w2, b2, w3, b3, w4, freq, decay)
    k_f = jnp.fft.rfft(k, axis=0)
    sk = skip.astype(jnp.float32)
    z = v.astype(jnp.float32)
    for n, gate in enumerate((x1, x2)):
        zf = jnp.fft.rfft(z, n=2 * L, axis=1)
        y = jnp.fft.irfft(zf * k_f[None, :, n, :], n=2 * L, axis=1)[:, :L]
        z = gate.astype(jnp.float32) * (y + z * sk[n])
    return z.astype(u.dtype)


def ab_mixer(h, w_in, cv_dw_w, cv_dw_b, cv_ln_g, cv_ln_b, hy_short_w, hy_short_b,
             hy_w1, hy_b1, hy_w2, hy_b2, hy_w3, hy_b3, hy_w4, hy_freq, hy_decay, hy_skip, w_out):
    u = h @ w_in
    y_a = conformer_conv(u[..., :2 * CONV_WIDTH], cv_dw_w, cv_dw_b, cv_ln_g, cv_ln_b)
    y_b = hyena(u[..., 2 * CONV_WIDTH:], hy_short_w, hy_short_b, hy_w1, hy_b1, hy_w2, hy_b2,
                hy_w3, hy_b3, hy_w4, hy_freq, hy_decay, hy_skip)
    return jnp.concatenate([y_a, y_b], axis=-1) @ w_out


def rope_partial(x):
    L = x.shape[1]
    inv = ROPE_THETA ** (-(jnp.arange(0, ROT_DIM, 2, dtype=jnp.float32) / ROT_DIM))
    ang = jnp.arange(L, dtype=jnp.float32)[:, None] * inv[None, :]
    cos = jnp.cos(ang)[None, :, None, :]
    sin = jnp.sin(ang)[None, :, None, :]
    xr = x[..., :ROT_DIM].astype(jnp.float32)
    a, b = xr[..., :ROT_DIM // 2], xr[..., ROT_DIM // 2:]
    rot = jnp.concatenate([a * cos - b * sin, b * cos + a * sin], axis=-1)
    return jnp.concatenate([rot.astype(x.dtype), x[..., ROT_DIM:]], axis=-1)


def window_attention(h, w_qkv, sink, w_o):
    B, L, _ = h.shape
    nb = L // BLOCK
    qkv = h @ w_qkv
    q = qkv[..., :N_HEADS * HEAD_DIM].reshape(B, L, N_HEADS, HEAD_DIM)
    k = qkv[..., N_HEADS * HEAD_DIM:(N_HEADS + N_KV_HEADS) * HEAD_DIM].reshape(B, L, N_KV_HEADS, HEAD_DIM)
    v = qkv[..., (N_HEADS + N_KV_HEADS) * HEAD_DIM:].reshape(B, L, N_KV_HEADS, HEAD_DIM)
    q = rope_partial(q)
    k = rope_partial(k)
    qb = q.reshape(B, nb, BLOCK, N_KV_HEADS, GROUP, HEAD_DIM)
    padw = ((0, 0), (BLOCK, BLOCK), (0, 0), (0, 0))
    kp = jnp.pad(k, padw).reshape(B, nb + 2, BLOCK, N_KV_HEADS, HEAD_DIM)
    vp = jnp.pad(v, padw).reshape(B, nb + 2, BLOCK, N_KV_HEADS, HEAD_DIM)
    kb = jnp.concatenate([kp[:, :-2], kp[:, 1:-1], kp[:, 2:]], axis=2)
    vb = jnp.concatenate([vp[:, :-2], vp[:, 1:-1], vp[:, 2:]], axis=2)
    s = jnp.einsum('bnqkgd,bnskd->bnkgqs', qb, kb,
                   preferred_element_type=jnp.float32) * (HEAD_DIM ** -0.5)
    qpos = jnp.arange(nb)[:, None] * BLOCK + jnp.arange(BLOCK)[None, :]
    kpos = jnp.arange(nb)[:, None] * BLOCK - BLOCK + jnp.arange(3 * BLOCK)[None, :]
    rel = kpos[:, None, :] - qpos[:, :, None]
    valid = (jnp.abs(rel) <= WINDOW) & (kpos >= 0)[:, None, :] & (kpos < L)[:, None, :]
    s = jnp.where(valid[None, :, None, None], s, -jnp.inf)
    sink_b = sink.astype(jnp.float32).reshape(N_KV_HEADS, GROUP)[None, None, :, :, None]
    m = jnp.maximum(jnp.max(s, axis=-1), sink_b)
    p = jnp.exp(s - m[..., None])
    denom = jnp.sum(p, axis=-1) + jnp.exp(sink_b - m)
    p = (p / denom[..., None]).astype(vb.dtype)
    o = jnp.einsum('bnkgqs,bnskd->bnqkgd', p, vb)
    return o.reshape(B, L, N_HEADS * HEAD_DIM) @ w_o


def sq_relu_mlp(h, w_up, w_down):
    return jnp.square(jax.nn.relu(h @ w_up)) @ w_down


def trunk(x, norm_mix, norm_mlp, norm_final, ab_w_in, ab_w_out, cv_dw_w, cv_dw_b, cv_ln_g, cv_ln_b,
          hy_short_w, hy_short_b, hy_w1, hy_b1, hy_w2, hy_b2, hy_w3, hy_b3, hy_w4, hy_freq,
          hy_decay, hy_skip, at_w_qkv, at_sink, at_w_o, mlp_w_up, mlp_w_down):
    for i in range(DEPTH):
        j = i // 2
        hn = rms_norm(x, norm_mix[i])
        if i % 2 == 0:
            x = x + ab_mixer(hn, ab_w_in[j], cv_dw_w[j], cv_dw_b[j], cv_ln_g[j], cv_ln_b[j],
                             hy_short_w[j], hy_short_b[j], hy_w1[j], hy_b1[j], hy_w2[j], hy_b2[j],
                             hy_w3[j], hy_b3[j], hy_w4[j], hy_freq[j], hy_decay[j], hy_skip[j],
                             ab_w_out[j])
        else:
            x = x + window_attention(hn, at_w_qkv[j], at_sink[j], at_w_o[j])
        hn = rms_norm(x, norm_mlp[i])
        x = x + sq_relu_mlp(hn, mlp_w_up[i], mlp_w_down[i])
    return rms_norm(x, norm_final)


def setup_inputs(seed: int = 0) -> dict:
    key = jax.random.key(seed)
    ks = jax.random.split(key, 32)
    f32 = jnp.float32

    def nrm(k, shape, scale):
        return jax.random.normal(k, shape, f32) * scale

    max_decay = math.log(HYENA_DECAY_TARGET) / HYENA_FAST_DECAY_PCT
    min_decay = math.log(HYENA_DECAY_TARGET) / HYENA_SLOW_DECAY_PCT
    base = jnp.tile(jnp.linspace(min_decay, max_decay, HYENA_WIDTH, dtype=f32), 2 * HYENA_ORDER)
    FH = HYENA_FILTER_HIDDEN
    return {
        'x_prompt': nrm(ks[0], (BATCH, SEQ, D_MODEL), 1.0),
        'x_sample': nrm(ks[1], (DEC_BATCH, DEC_SEQ, D_MODEL), 1.0),
        'norm_mix': 1.0 + nrm(ks[2], (DEPTH, D_MODEL), 0.02),
        'norm_mlp': 1.0 + nrm(ks[3], (DEPTH, D_MODEL), 0.02),
        'norm_final': 1.0 + nrm(ks[4], (D_MODEL,), 0.02),
        'ab_w_in': nrm(ks[5], (N_AB, D_MODEL, AB_IN_WIDTH), D_MODEL ** -0.5),
        'ab_w_out': nrm(ks[6], (N_AB, D_MODEL, D_MODEL), D_MODEL ** -0.5),
        'cv_dw_w': nrm(ks[7], (N_AB, CONV_KERNEL, CONV_WIDTH), CONV_KERNEL ** -0.5),
        'cv_dw_b': nrm(ks[8], (N_AB, CONV_WIDTH), 0.02),
        'cv_ln_g': 1.0 + nrm(ks[9], (N_AB, CONV_WIDTH), 0.02),
        'cv_ln_b': nrm(ks[10], (N_AB, CONV_WIDTH), 0.02),
        'hy_short_w': nrm(ks[11], (N_AB, SHORT_KERNEL, HYENA_IN), SHORT_KERNEL ** -0.5),
        'hy_short_b': nrm(ks[12], (N_AB, HYENA_IN), 0.02),
        'hy_w1': nrm(ks[13], (N_AB, HYENA_EMB_DIM, FH), HYENA_EMB_DIM ** -0.5),
        'hy_b1': nrm(ks[14], (N_AB, FH), 0.02),
        'hy_w2': nrm(ks[15], (N_AB, FH, FH), FH ** -0.5),
        'hy_b2': nrm(ks[16], (N_AB, FH), 0.02),
        'hy_w3': nrm(ks[17], (N_AB, FH, FH), FH ** -0.5),
        'hy_b3': nrm(ks[18], (N_AB, FH), 0.02),
        'hy_w4': nrm(ks[19], (N_AB, FH, N_FILTER_CH), FH ** -0.5),
        'hy_freq': 1.0 + nrm(ks[20], (N_AB, 3, FH), 0.1),
        'hy_decay': base[None, :] * (1.0 + nrm(ks[21], (N_AB, N_FILTER_CH), 0.05)),
        'hy_skip': nrm(ks[22], (N_AB, HYENA_ORDER, HYENA_WIDTH), 0.5),
        'at_w_qkv': nrm(ks[23], (N_C, D_MODEL, QKV_WIDTH), D_MODEL ** -0.5),
        'at_sink': nrm(ks[24], (N_C, N_HEADS), 0.5),
        'at_w_o': nrm(ks[25], (N_C, N_HEADS * HEAD_DIM, D_MODEL), (N_HEADS * HEAD_DIM) ** -0.5),
        'mlp_w_up': nrm(ks[26], (DEPTH, D_MODEL, D_FF), D_MODEL ** -0.5),
        'mlp_w_down': nrm(ks[27], (DEPTH, D_FF, D_MODEL), D_FF ** -0.5),
    }


def reference(x_prompt, x_sample, norm_mix, norm_mlp, norm_final, ab_w_in, ab_w_out, cv_dw_w, cv_dw_b,
              cv_ln_g, cv_ln_b, hy_short_w, hy_short_b, hy_w1, hy_b1, hy_w2, hy_b2, hy_w3, hy_b3, hy_w4,
              hy_freq, hy_decay, hy_skip, at_w_qkv, at_sink, at_w_o, mlp_w_up, mlp_w_down):
    weights = (norm_mix, norm_mlp, norm_final, ab_w_in, ab_w_out, cv_dw_w, cv_dw_b, cv_ln_g, cv_ln_b,
               hy_short_w, hy_short_b, hy_w1, hy_b1, hy_w2, hy_b2, hy_w3, hy_b3, hy_w4, hy_freq,
               hy_decay, hy_skip, at_w_qkv, at_sink, at_w_o, mlp_w_up, mlp_w_down)
    y_prompt = trunk(x_prompt, *weights)
    y_sample = trunk(x_sample, *weights)
    return (y_prompt, y_sample)
```

```python
import functools
import math

import numpy as np
import jax
import jax.numpy as jnp
from jax import lax
from jax.experimental import pallas as pl
from jax.experimental.pallas import tpu as pltpu

F32 = jnp.float32
BF16 = jnp.bfloat16

D_MODEL = 1024
CONV_WIDTH = 512
CONV_KERNEL = 31
HYENA_WIDTH = 512
HYENA_IN = 3 * HYENA_WIDTH
HYENA_EMB_DIM = 33
FILTER_HIDDEN = 64
N_HEADS = 16
N_KV_HEADS = 4
HEAD_DIM = 64
ROT_DIM = 16
ROPE_THETA = 500000.0
WINDOW = 128
D_FF = 4 * D_MODEL
NORM_EPS = 1e-5
LN_EPS = 1e-5
FILTER_EPS = 1e-6

LANES = 128
DFT_P = 128
DFT_PH = DFT_P // 2
VMEM_LIMIT = 56 * 1024 * 1024


def _params(*sem):
    return pltpu.CompilerParams(dimension_semantics=sem, vmem_limit_bytes=VMEM_LIMIT)


def _resident(shape):
    nd = len(shape)
    return pl.BlockSpec(shape, lambda *_: (0,) * nd, pipeline_mode=pl.Buffered(1))


def _rms(x, g):
    return x * lax.rsqrt(jnp.mean(x * x, axis=-1, keepdims=True) + NORM_EPS) * g


IN_TM = 512


def _in_proj_kernel(x_ref, g_ref, w_ref, uc_ref, uh_ref):
    hn = _rms(x_ref[...], g_ref[...]).astype(BF16)
    uc_ref[...] = jnp.dot(hn, w_ref[:, :2 * CONV_WIDTH], preferred_element_type=F32)
    uh_ref[...] = jnp.dot(hn, w_ref[:, 2 * CONV_WIDTH:], preferred_element_type=F32)


def _in_proj(x, g, w):
    t = x.shape[0]
    n_out = w.shape[1]
    return pl.pallas_call(
        _in_proj_kernel,
        grid=(t // IN_TM,),
        in_specs=[pl.BlockSpec((IN_TM, D_MODEL), lambda i: (i, 0)),
                  _resident((1, D_MODEL)), _resident((D_MODEL, n_out))],
        out_specs=[pl.BlockSpec((IN_TM, 2 * CONV_WIDTH), lambda i: (i, 0)),
                   pl.BlockSpec((IN_TM, HYENA_IN), lambda i: (i, 0))],
        out_shape=[jax.ShapeDtypeStruct((t, 2 * CONV_WIDTH), F32),
                   jax.ShapeDtypeStruct((t, HYENA_IN), F32)],
        compiler_params=_params("parallel"),
        name="in_proj",
    )(x, g, w)


CC_TR = 512
CC_HALO = 16
CC_RC = 32


def _cconv_kernel(main_ref, prev_ref, next_ref, w_ref, b_ref, g_ref, beta_ref, o_ref, hbuf):
    i = pl.program_id(1)
    last = pl.num_programs(1) - 1

    def glu(u):
        return u[:, :CONV_WIDTH] * jax.nn.sigmoid(u[:, CONV_WIDTH:])

    hbuf[CC_HALO:CC_HALO + CC_TR, :] = glu(main_ref[...])
    hbuf[0:CC_HALO, :] = jnp.where(i > 0, glu(prev_ref[...]), 0.0)
    hbuf[CC_HALO + CC_TR:, :] = jnp.where(i < last, glu(next_ref[...]), 0.0)
    shift = CC_HALO - CONV_KERNEL // 2
    for r in range(CC_TR // CC_RC):
        acc = jnp.broadcast_to(b_ref[...], (CC_RC, CONV_WIDTH))
        for j in range(CONV_KERNEL):
            lo = r * CC_RC + j + shift
            acc = acc + hbuf[lo:lo + CC_RC, :] * w_ref[j:j + 1, :]
        mu = jnp.mean(acc, axis=-1, keepdims=True)
        d = acc - mu
        var = jnp.mean(d * d, axis=-1, keepdims=True)
        y = d * lax.rsqrt(var + LN_EPS) * g_ref[...] + beta_ref[...]
        o_ref[r * CC_RC:(r + 1) * CC_RC, :] = (y * jax.nn.sigmoid(y)).astype(o_ref.dtype)


def _conformer_conv(uc, w, b, g, beta):
    bsz, seq, _ = uc.shape
    nh = CC_TR // CC_HALO
    n_halo = seq // CC_HALO
    return pl.pallas_call(
        _cconv_kernel,
        grid=(bsz, seq // CC_TR),
        in_specs=[
            pl.BlockSpec((None, CC_TR, 2 * CONV_WIDTH), lambda bi, i: (bi, i, 0)),
            pl.BlockSpec((None, CC_HALO, 2 * CONV_WIDTH),
                         lambda bi, i: (bi, jnp.maximum(i * nh - 1, 0), 0)),
            pl.BlockSpec((None, CC_HALO, 2 * CONV_WIDTH),
                         lambda bi, i: (bi, jnp.minimum((i + 1) * nh, n_halo - 1), 0)),
            _resident((CONV_KERNEL, CONV_WIDTH)), _resident((1, CONV_WIDTH)),
            _resident((1, CONV_WIDTH)), _resident((1, CONV_WIDTH)),
        ],
        out_specs=pl.BlockSpec((None, CC_TR, CONV_WIDTH), lambda bi, i: (bi, i, 0)),
        out_shape=jax.ShapeDtypeStruct((bsz, seq, CONV_WIDTH), BF16),
        scratch_shapes=[pltpu.VMEM((CC_TR + 2 * CC_HALO, CONV_WIDTH), F32)],
        compiler_params=_params("parallel", "parallel"),
        name="conformer_conv",
    )(uc, uc, uc, w, b, g, beta)


SC_TR = 512
SC_HALO = 8


def _short_conv_kernel(main_ref, prev_ref, next_ref, w_ref, b_ref, x1_ref, x2_ref, v_ref, buf):
    i = pl.program_id(1)
    last = pl.num_programs(1) - 1
    buf[SC_HALO:SC_HALO + SC_TR, :] = main_ref[...]
    buf[0:SC_HALO, :] = jnp.where(i > 0, prev_ref[...], 0.0)
    buf[SC_HALO + SC_TR:, :] = jnp.where(i < last, next_ref[...], 0.0)
    for n, o_ref in enumerate((x1_ref, x2_ref, v_ref)):
        cs = slice(n * HYENA_WIDTH, (n + 1) * HYENA_WIDTH)
        o_ref[...] = (buf[SC_HALO - 1:SC_HALO - 1 + SC_TR, cs] * w_ref[0:1, cs]
                      + buf[SC_HALO:SC_HALO + SC_TR, cs] * w_ref[1:2, cs]
                      + buf[SC_HALO + 1:SC_HALO + 1 + SC_TR, cs] * w_ref[2:3, cs]
                      + b_ref[:, cs])


def _short_conv(uh, w, b):
    bsz, seq, _ = uh.shape
    nh = SC_TR // SC_HALO
    n_halo = seq // SC_HALO
    out = jax.ShapeDtypeStruct((bsz, seq, HYENA_WIDTH), F32)
    ospec = pl.BlockSpec((None, SC_TR, HYENA_WIDTH), lambda bi, i: (bi, i, 0))
    return pl.pallas_call(
        _short_conv_kernel,
        grid=(bsz, seq // SC_TR),
        in_specs=[
            pl.BlockSpec((None, SC_TR, HYENA_IN), lambda bi, i: (bi, i, 0)),
            pl.BlockSpec((None, SC_HALO, HYENA_IN),
                         lambda bi, i: (bi, jnp.maximum(i * nh - 1, 0), 0)),
            pl.BlockSpec((None, SC_HALO, HYENA_IN),
                         lambda bi, i: (bi, jnp.minimum((i + 1) * nh, n_halo - 1), 0)),
            _resident((3, HYENA_IN)), _resident((1, HYENA_IN)),
        ],
        out_specs=[ospec, ospec, ospec],
        out_shape=[out, out, out],
        scratch_shapes=[pltpu.VMEM((SC_TR + 2 * SC_HALO, HYENA_IN), F32)],
        compiler_params=_params("parallel", "parallel"),
        name="short_conv",
    )(uh, uh, uh, w, b)


FM_TL = 512
FEAT = 64
MASK_COL = HYENA_EMB_DIM
N_FILT = 2 * HYENA_WIDTH


@functools.lru_cache(maxsize=None)
def _filter_features(seq):
    t = np.linspace(0.0, 1.0, seq, dtype=np.float32).astype(np.float64)
    bands = (HYENA_EMB_DIM - 1) // 2
    w = (2.0 * math.pi * np.arange(seq, dtype=np.float32) / np.float32(seq)).astype(np.float64)
    f = np.linspace(1e-4, bands - 1, bands, dtype=np.float32).astype(np.float64)
    fw = w[:, None] * f[None, :]
    z = np.concatenate([t[:, None], np.cos(fw), -np.sin(fw)], axis=-1)
    fwd = np.zeros((seq, FEAT), np.float64)
    fwd[:, :HYENA_EMB_DIM] = z
    fwd[:, MASK_COL] = 1.0
    bwd = np.zeros((seq, FEAT), np.float64)
    bwd[1:, :HYENA_EMB_DIM] = z[:0:-1]
    bwd[1:, MASK_COL] = 1.0
    return np.concatenate([fwd, bwd], axis=-1).astype(np.float32)


def _filter_mlp_kernel(z_ref, w1_ref, b1_ref, w2_ref, b2_ref, w3_ref, b3_ref, w4_ref, fr_ref,
                       dec_ref, k_ref, ssq_ref):
    hp = lax.Precision.HIGHEST
    z = z_ref[...]
    h = jnp.sin(fr_ref[0:1, :] * (jnp.dot(z, w1_ref[...], precision=hp,
                                          preferred_element_type=F32) + b1_ref[...]))
    h = jnp.sin(fr_ref[1:2, :] * (jnp.dot(h, w2_ref[...], precision=hp,
                                          preferred_element_type=F32) + b2_ref[...]))
    h = jnp.sin(fr_ref[2:3, :] * (jnp.dot(h, w3_ref[...], precision=hp,
                                          preferred_element_type=F32) + b3_ref[...]))
    ssq = jnp.zeros((1, N_FILT), F32)
    for d in range(2):
        k = jnp.dot(h, w4_ref[d], precision=hp, preferred_element_type=F32)
        t = z[:, d * FEAT:d * FEAT + 1]
        mask = z[:, d * FEAT + MASK_COL:d * FEAT + MASK_COL + 1]
        k = k * jnp.exp(-t * jnp.abs(dec_ref[d])) * mask
        k_ref[d] = k
        ssq = ssq + jnp.sum(k * k, axis=0, keepdims=True)

    @pl.when(pl.program_id(0) == 0)
    def _():
        ssq_ref[...] = jnp.zeros_like(ssq_ref)

    ssq_ref[...] += ssq


def _filter_mlp(seq, w1, b1, w2, b2, w3, b3, w4, freq, decay):
    fh = FILTER_HIDDEN

    def blockdiag(w):
        z = jnp.zeros_like(w)
        return jnp.concatenate([jnp.concatenate([w, z], 1), jnp.concatenate([z, w], 1)], 0)

    w1p = jnp.zeros((FEAT, fh), F32).at[:HYENA_EMB_DIM].set(w1)
    w4r = w4.reshape(fh, 2, 2, HYENA_WIDTH)
    zeros = jnp.zeros((fh, N_FILT), F32)
    w4f = jnp.concatenate([w4r[:, :, 0].reshape(fh, N_FILT), zeros], 0)
    w4b = jnp.concatenate([zeros, w4r[:, :, 1].reshape(fh, N_FILT)], 0)
    dec = decay.reshape(2, 2, HYENA_WIDTH)
    args = (
        jnp.asarray(_filter_features(seq)),
        blockdiag(w1p), jnp.tile(b1, 2)[None], blockdiag(w2), jnp.tile(b2, 2)[None],
        blockdiag(w3), jnp.tile(b3, 2)[None],
        jnp.stack([w4f, w4b]),
        jnp.tile(freq, (1, 2)),
        jnp.stack([dec[:, 0].reshape(1, N_FILT), dec[:, 1].reshape(1, N_FILT)]),
    )
    in_specs = [pl.BlockSpec((FM_TL, 2 * FEAT), lambda i: (i, 0))]
    in_specs += [_resident(a.shape) for a in args[1:]]
    return pl.pallas_call(
        _filter_mlp_kernel,
        grid=(seq // FM_TL,),
        in_specs=in_specs,
        out_specs=[pl.BlockSpec((2, FM_TL, N_FILT), lambda i: (0, i, 0)),
                   pl.BlockSpec((1, N_FILT), lambda i: (0, 0))],
        out_shape=[jax.ShapeDtypeStruct((2, seq, N_FILT), F32),
                   jax.ShapeDtypeStruct((1, N_FILT), F32)],
        compiler_params=_params("arbitrary"),
        name="filter_mlp",
    )(*args)


@functools.lru_cache(maxsize=None)
def _dft_tables(q):
    n = DFT_P * q
    c = np.arange(DFT_P)[:, None]
    a = np.arange(DFT_P)[None, :]
    b = np.arange(q)[:, None, None]
    ang = -2.0 * np.pi * ((c * (q * a + b)) % n) / n
    fr, fi = np.cos(ang), np.sin(ang)
    d = np.arange(q)[:, None]
    ang2 = -2.0 * np.pi * ((d * np.arange(q)[None, :]) % q) / q
    gr, gi = np.cos(ang2), np.sin(ang2)
    g2 = np.block([[gr, -gi], [gi, gr]])
    g2i = np.block([[gr, gi], [-gi, gr]])
    return (fr.astype(np.float32), fi.astype(np.float32),
            g2.astype(np.float32), g2i.astype(np.float32))


def _dft_mats(q):
    fr, fi, g2, g2i = (jnp.asarray(t) for t in _dft_tables(q))
    frh, fih = fr[:, :, :DFT_PH], fi[:, :, :DFT_PH]
    m1 = jnp.concatenate([jnp.concatenate([frh, -fih], 2), jnp.concatenate([fih, frh], 2)], 1)
    m1f = jnp.concatenate([fr, fi], 1)
    hr, hi = jnp.swapaxes(frh, 1, 2), -jnp.swapaxes(fih, 1, 2)
    m3 = jnp.concatenate([jnp.concatenate([hr, -hi], 2), jnp.concatenate([hi, hr], 2)], 1)
    return (m1.astype(BF16), m1f.astype(BF16), m3.astype(BF16),
            g2.astype(BF16), g2i.astype(BF16))


S1_NB = 8


def _dft_s1_kernel(x_ref, m_ref, ar_ref, ai_ref, *, cw):
    for j in range(S1_NB):
        seg = slice(j * cw, (j + 1) * cw)
        r = jnp.dot(m_ref[j], x_ref[:, seg].astype(BF16), preferred_element_type=F32)
        ar_ref[:, seg] = r[:DFT_P].astype(BF16)
        ai_ref[:, seg] = r[DFT_P:].astype(BF16)


def _dft_stage1(x, m, q, cw):
    g = x.shape[0]
    w = S1_NB * cw
    out = jax.ShapeDtypeStruct((g, DFT_P, q * cw), BF16)
    spec = pl.BlockSpec((None, DFT_P, w), lambda gi, j: (gi, 0, j))
    return pl.pallas_call(
        functools.partial(_dft_s1_kernel, cw=cw),
        grid=(g, q // S1_NB),
        in_specs=[spec, pl.BlockSpec((S1_NB, 2 * DFT_P, DFT_P), lambda gi, j: (j, 0, 0))],
        out_specs=[spec, spec],
        out_shape=[out, out],
        compiler_params=_params("parallel", "parallel"),
        name="dft_stage1",
    )(x, m)


S2_NC = 8


def _filter_s2_kernel(ar_ref, ai_ref, g2_ref, ssq_ref, kr_ref, ki_ref, *, q, inv_n):
    scale = lax.rsqrt(ssq_ref[...] + FILTER_EPS) * inv_n
    for c in range(S2_NC):
        a = jnp.concatenate([ar_ref[c], ai_ref[c]], axis=0)
        x = jnp.dot(g2_ref[...], a, preferred_element_type=F32)
        kr_ref[c] = x[:q] * scale
        ki_ref[c] = x[q:] * scale


def _filter_stage2(ar, ai, g2, ssq, q):
    spec = pl.BlockSpec((S2_NC, q, N_FILT), lambda i: (i, 0, 0))
    out = jax.ShapeDtypeStruct((DFT_P, q, N_FILT), F32)
    return pl.pallas_call(
        functools.partial(_filter_s2_kernel, q=q, inv_n=1.0 / (DFT_P * q)),
        grid=(DFT_P // S2_NC,),
        in_specs=[spec, spec, _resident((2 * q, 2 * q)), _resident((1, N_FILT))],
        out_specs=[spec, spec],
        out_shape=[out, out],
        compiler_params=_params("parallel"),
        name="filter_stage2",
    )(ar, ai, g2, ssq)


def _dft_mid_kernel(ar_ref, ai_ref, g2_ref, g2i_ref, kr_ref, ki_ref, yr_ref, yi_ref, *, q):
    for c in range(S2_NC):
        a = jnp.concatenate([ar_ref[c], ai_ref[c]], axis=0)
        x = jnp.dot(g2_ref[...], a, preferred_element_type=F32)
        xr, xi = x[:q], x[q:]
        kr, ki = kr_ref[c], ki_ref[c]
        z = jnp.concatenate([xr * kr - xi * ki, xr * ki + xi * kr], axis=0).astype(BF16)
        y = jnp.dot(g2i_ref[...], z, preferred_element_type=F32)
        yr_ref[c] = y[:q].astype(BF16)
        yi_ref[c] = y[q:].astype(BF16)


def _dft_mid(ar, ai, g2, g2i, kr, ki, order, q):
    pairs = ar.shape[0]
    cw = HYENA_WIDTH
    spec = pl.BlockSpec((None, S2_NC, q, cw), lambda i, p: (p, i, 0, 0))
    kspec = pl.BlockSpec((S2_NC, q, cw), lambda i, p: (i, 0, order))
    out = jax.ShapeDtypeStruct(ar.shape, BF16)
    return pl.pallas_call(
        functools.partial(_dft_mid_kernel, q=q),
        grid=(DFT_P // S2_NC, pairs),
        in_specs=[spec, spec, _resident((2 * q, 2 * q)), _resident((2 * q, 2 * q)), kspec, kspec],
        out_specs=[spec, spec],
        out_shape=[out, out],
        compiler_params=_params("parallel", "parallel"),
        name="dft_mid",
    )(ar, ai, g2, g2i, kr, ki)


def _dft_s3_kernel(yr_ref, yi_ref, m_ref, gate_ref, z_ref, sk_ref, o_ref, *, cw):
    for j in range(S1_NB):
        seg = slice(j * cw, (j + 1) * cw)
        yc = jnp.concatenate([yr_ref[:, seg], yi_ref[:, seg]], axis=0)
        y = jnp.dot(m_ref[j], yc, preferred_element_type=F32)
        o_ref[:, seg] = (gate_ref[:, seg] * (y + z_ref[:, seg] * sk_ref[...])).astype(o_ref.dtype)


def _dft_stage3(yr, yi, m3, gate, z, sk, q, out_dtype):
    pairs = yr.shape[0]
    cw = HYENA_WIDTH
    w = S1_NB * cw
    spec = pl.BlockSpec((None, DFT_P, w), lambda p, j: (p, 0, j))
    return pl.pallas_call(
        functools.partial(_dft_s3_kernel, cw=cw),
        grid=(pairs, q // S1_NB),
        in_specs=[spec, spec, pl.BlockSpec((S1_NB, DFT_P, 2 * DFT_P), lambda p, j: (j, 0, 0)),
                  spec, spec, _resident((1, cw))],
        out_specs=spec,
        out_shape=jax.ShapeDtypeStruct((pairs, DFT_P, q * cw), out_dtype),
        compiler_params=_params("parallel", "parallel"),
        name="dft_stage3",
    )(yr, yi, m3, gate, z, sk)


def _hyena(uh, short_w, short_b, w1, b1, w2, b2, w3, b3, w4, freq, decay, skip):
    bsz, seq, _ = uh.shape
    pairs = bsz // 2
    q = 2 * seq // DFT_P
    cw = HYENA_WIDTH
    m1, m1f, m3, g2, g2i = _dft_mats(q)

    k_time, ssq = _filter_mlp(seq, w1, b1, w2, b2, w3, b3, w4, freq, decay)
    afr, afi = _dft_stage1(k_time.reshape(1, DFT_P, q * N_FILT), m1f, q, N_FILT)
    kr, ki = _filter_stage2(afr.reshape(DFT_P, q, N_FILT), afi.reshape(DFT_P, q, N_FILT),
                            g2, ssq, q)

    x1, x2, v = _short_conv(uh, short_w, short_b[None])
    pair_view = (pairs, DFT_P, q * cw)
    z = v.reshape(pair_view)
    gates = (x1.reshape(pair_view), x2.reshape(pair_view))
    for n in range(2):
        ar, ai = _dft_stage1(z, m1, q, cw)
        yr, yi = _dft_mid(ar.reshape(pairs, DFT_P, q, cw), ai.reshape(pairs, DFT_P, q, cw),
                          g2, g2i, kr, ki, n, q)
        z = _dft_stage3(yr.reshape(pair_view), yi.reshape(pair_view), m3, gates[n], z,
                        skip[n][None], q, F32 if n == 0 else BF16)
    return z.reshape(bsz, seq, cw)


PM_TM = 512
PM_FC = 1024


def _proj_mlp_kernel(*refs, n_parts, final):
    x_ref = refs[0]
    part_refs = refs[1:1 + n_parts]
    wo_ref, gm_ref, wup_ref, wdn_ref = refs[1 + n_parts:5 + n_parts]
    gf_ref = refs[5 + n_parts] if final else None
    o_ref = refs[-1]
    mixed = jnp.concatenate([p_ref[...] for p_ref in part_refs], axis=1)
    x = x_ref[...] + jnp.dot(mixed, wo_ref[...], preferred_element_type=F32)
    hn = _rms(x, gm_ref[...]).astype(BF16)
    for ch in range(D_FF // PM_FC):
        cs = slice(ch * PM_FC, (ch + 1) * PM_FC)
        h = jnp.dot(hn, wup_ref[:, cs], preferred_element_type=F32)
        h = jnp.square(jnp.maximum(h, 0.0)).astype(BF16)
        x = x + jnp.dot(h, wdn_ref[cs, :], preferred_element_type=F32)
    if final:
        x = _rms(x, gf_ref[...])
    o_ref[...] = x


def _proj_mlp(x, parts, wo, gm, wup, wdn, gf=None):
    t = x.shape[0]
    final = gf is not None
    row = lambda i: (i, 0)
    in_specs = [pl.BlockSpec((PM_TM, D_MODEL), row)]
    in_specs += [pl.BlockSpec((PM_TM, p.shape[1]), row) for p in parts]
    in_specs += [_resident(wo.shape), _resident((1, D_MODEL)), _resident(wup.shape),
                 _resident(wdn.shape)]
    args = [x, *parts, wo, gm, wup, wdn]
    if final:
        in_specs.append(_resident((1, D_MODEL)))
        args.append(gf)
    return pl.pallas_call(
        functools.partial(_proj_mlp_kernel, n_parts=len(parts), final=final),
        grid=(t // PM_TM,),
        in_specs=in_specs,
        out_specs=pl.BlockSpec((PM_TM, D_MODEL), row),
        out_shape=jax.ShapeDtypeStruct((t, D_MODEL), F32),
        compiler_params=_params("parallel"),
        name="proj_mlp",
    )(*args)


QK_TM = 512
Q_WIDTH = N_HEADS * HEAD_DIM
KV_WIDTH = N_KV_HEADS * HEAD_DIM
ROT_HALF = ROT_DIM // 2


@functools.lru_cache(maxsize=None)
def _rope_tables(seq):
    inv = ROPE_THETA ** (-(np.arange(0, ROT_DIM, 2, dtype=np.float64) / ROT_DIM))
    ang = np.arange(seq, dtype=np.float64)[:, None] * inv[None, :]
    cos = np.ones((seq, HEAD_DIM))
    s_lo = np.zeros((seq, HEAD_DIM))
    s_hi = np.zeros((seq, HEAD_DIM))
    cos[:, :ROT_HALF] = np.cos(ang)
    cos[:, ROT_HALF:ROT_DIM] = np.cos(ang)
    s_lo[:, ROT_HALF:ROT_DIM] = np.sin(ang)
    s_hi[:, :ROT_HALF] = -np.sin(ang)
    rep = LANES // HEAD_DIM
    return np.stack([np.tile(cos, (1, rep)), np.tile(s_lo, (1, rep)),
                     np.tile(s_hi, (1, rep))]).astype(np.float32)


def _qkv_kernel(x_ref, g_ref, w_ref, rope_ref, q_ref, k_ref, v_ref):
    hn = _rms(x_ref[...], g_ref[...]).astype(BF16)
    cos, s_lo, s_hi = rope_ref[0], rope_ref[1], rope_ref[2]
    nq = Q_WIDTH // LANES
    for blk in range((Q_WIDTH + KV_WIDTH) // LANES):
        t = jnp.dot(hn, w_ref[:, blk * LANES:(blk + 1) * LANES], preferred_element_type=F32)
        r = (t * cos + pltpu.roll(t, ROT_HALF, axis=1) * s_lo
             + pltpu.roll(t, LANES - ROT_HALF, axis=1) * s_hi)
        if blk < nq:
            q_ref[:, blk * LANES:(blk + 1) * LANES] = (r * HEAD_DIM ** -0.5).astype(BF16)
        else:
            k_ref[:, (blk - nq) * LANES:(blk - nq + 1) * LANES] = r.astype(BF16)
    v_ref[...] = jnp.dot(hn, w_ref[:, Q_WIDTH + KV_WIDTH:],
                         preferred_element_type=F32).astype(BF16)


def _qkv(x, g, w, seq):
    t = x.shape[0]
    per_seq = seq // QK_TM
    row = lambda i: (i, 0)
    return pl.pallas_call(
        _qkv_kernel,
        grid=(t // QK_TM,),
        in_specs=[pl.BlockSpec((QK_TM, D_MODEL), row), _resident((1, D_MODEL)),
                  _resident(w.shape),
                  pl.BlockSpec((3, QK_TM, LANES), lambda i: (0, i % per_seq, 0))],
        out_specs=[pl.BlockSpec((QK_TM, Q_WIDTH), row), pl.BlockSpec((QK_TM, KV_WIDTH), row),
                   pl.BlockSpec((QK_TM, KV_WIDTH), row)],
        out_shape=[jax.ShapeDtypeStruct((t, Q_WIDTH), BF16),
                   jax.ShapeDtypeStruct((t, KV_WIDTH), BF16),
                   jax.ShapeDtypeStruct((t, KV_WIDTH), BF16)],
        compiler_params=_params("parallel"),
        name="qkv_rope",
    )(x, g, w, jnp.asarray(_rope_tables(seq)))


AT_TQ = 512
AT_BLK = WINDOW
AT_KEYS = 3 * AT_BLK
GROUP = N_HEADS // N_KV_HEADS


def _attn_kernel(q_ref, k_ref, kp_ref, kn_ref, v_ref, vp_ref, vn_ref, sink_ref, o_ref,
                 kbuf, vbuf, *, seq):
    i = pl.program_id(1)
    kbuf[0:AT_BLK, :] = kp_ref[...]
    kbuf[AT_BLK:AT_BLK + AT_TQ, :] = k_ref[...]
    kbuf[AT_BLK + AT_TQ:, :] = kn_ref[...]
    vbuf[0:AT_BLK, :] = vp_ref[...]
    vbuf[AT_BLK:AT_BLK + AT_TQ, :] = v_ref[...]
    vbuf[AT_BLK + AT_TQ:, :] = vn_ref[...]
    lw = lax.broadcasted_iota(jnp.int32, (AT_KEYS, LANES), 1) < HEAD_DIM
    for jb in range(AT_TQ // AT_BLK):
        r0 = jb * AT_BLK
        qpos = i * AT_TQ + r0 + lax.broadcasted_iota(jnp.int32, (AT_BLK, AT_KEYS), 0)
        kpos = i * AT_TQ + r0 - AT_BLK + lax.broadcasted_iota(jnp.int32, (AT_BLK, AT_KEYS), 1)
        valid = (jnp.abs(kpos - qpos) <= WINDOW) & (kpos >= 0) & (kpos < seq)
        for kvh in range(N_KV_HEADS):
            tile = kvh // 2
            kt = kbuf[r0:r0 + AT_KEYS, tile * LANES:(tile + 1) * LANES]
            vt = vbuf[r0:r0 + AT_KEYS, tile * LANES:(tile + 1) * LANES]
            zero = jnp.zeros_like(kt)
            if kvh % 2 == 0:
                k_lo, v_lo = jnp.where(lw, kt, zero), jnp.where(lw, vt, zero)
                k_hi = jnp.concatenate([zero[:, :HEAD_DIM], kt[:, :HEAD_DIM]], axis=1)
                v_hi = jnp.concatenate([zero[:, :HEAD_DIM], vt[:, :HEAD_DIM]], axis=1)
            else:
                k_hi, v_hi = jnp.where(lw, zero, kt), jnp.where(lw, zero, vt)
                k_lo = jnp.concatenate([kt[:, HEAD_DIM:], zero[:, :HEAD_DIM]], axis=1)
                v_lo = jnp.concatenate([vt[:, HEAD_DIM:], zero[:, :HEAD_DIM]], axis=1)
            for pair in range(GROUP // 2):
                col = (kvh * GROUP + 2 * pair) * HEAD_DIM
                qt = q_ref[r0:r0 + AT_BLK, col:col + LANES]
                acc = jnp.zeros((AT_BLK, LANES), F32)
                for half, (kx, vx) in enumerate(((k_lo, v_lo), (k_hi, v_hi))):
                    head = kvh * GROUP + 2 * pair + half
                    s = lax.dot_general(qt, kx, (((1,), (1,)), ((), ())),
                                        preferred_element_type=F32)
                    s = jnp.where(valid, s, -jnp.inf)
                    sink = sink_ref[0:1, head:head + 1]
                    m = jnp.maximum(jnp.max(s, axis=-1, keepdims=True), sink)
                    p = jnp.exp(s - m)
                    denom = jnp.sum(p, axis=-1, keepdims=True) + jnp.exp(sink - m)
                    o = jnp.dot(p.astype(BF16), vx, preferred_element_type=F32)
                    acc = acc + o / denom
                o_ref[r0:r0 + AT_BLK, col:col + LANES] = acc.astype(BF16)


def _attention(q, k, v, sink, bsz, seq):
    nb = AT_TQ // AT_BLK
    n_blk = seq // AT_BLK
    main = lambda w: pl.BlockSpec((None, AT_TQ, w), lambda b, i: (b, i, 0))
    prev = pl.BlockSpec((None, AT_BLK, KV_WIDTH), lambda b, i: (b, jnp.maximum(i * nb - 1, 0), 0))
    nxt = pl.BlockSpec((None, AT_BLK, KV_WIDTH),
                       lambda b, i: (b, jnp.minimum((i + 1) * nb, n_blk - 1), 0))
    q3 = q.reshape(bsz, seq, Q_WIDTH)
    k3 = k.reshape(bsz, seq, KV_WIDTH)
    v3 = v.reshape(bsz, seq, KV_WIDTH)
    out = pl.pallas_call(
        functools.partial(_attn_kernel, seq=seq),
        grid=(bsz, seq // AT_TQ),
        in_specs=[main(Q_WIDTH), main(KV_WIDTH), prev, nxt, main(KV_WIDTH), prev, nxt,
                  _resident((1, N_HEADS))],
        out_specs=main(Q_WIDTH),
        out_shape=jax.ShapeDtypeStruct((bsz, seq, Q_WIDTH), BF16),
        scratch_shapes=[pltpu.VMEM((AT_TQ + 2 * AT_BLK, KV_WIDTH), BF16),
                        pltpu.VMEM((AT_TQ + 2 * AT_BLK, KV_WIDTH), BF16)],
        compiler_params=_params("parallel", "parallel"),
        name="window_attention",
    )(q3, k3, k3, k3, v3, v3, v3, sink)
    return out.reshape(bsz * seq, Q_WIDTH)


def _trunk(x, norm_mix, norm_mlp, norm_final, ab_w_in, ab_w_out, cv_dw_w, cv_dw_b, cv_ln_g,
           cv_ln_b, hy_short_w, hy_short_b, hy_w1, hy_b1, hy_w2, hy_b2, hy_w3, hy_b3, hy_w4,
           hy_freq, hy_decay, hy_skip, at_w_qkv, at_sink, at_w_o, mlp_w_up, mlp_w_down):
    bsz, seq, _ = x.shape
    t = bsz * seq
    x2 = x.reshape(t, D_MODEL)

    uc, uh = _in_proj(x2, norm_mix[0][None], ab_w_in[0].astype(BF16))
    y_a = _conformer_conv(uc.reshape(bsz, seq, 2 * CONV_WIDTH), cv_dw_w[0], cv_dw_b[0][None],
                          cv_ln_g[0][None], cv_ln_b[0][None])
    y_b = _hyena(uh.reshape(bsz, seq, HYENA_IN), hy_short_w[0], hy_short_b[0], hy_w1[0], hy_b1[0],
                 hy_w2[0], hy_b2[0], hy_w3[0], hy_b3[0], hy_w4[0], hy_freq[0], hy_decay[0],
                 hy_skip[0])
    x2 = _proj_mlp(x2, [y_a.reshape(t, CONV_WIDTH), y_b.reshape(t, HYENA_WIDTH)],
                   ab_w_out[0].astype(BF16), norm_mlp[0][None], mlp_w_up[0].astype(BF16),
                   mlp_w_down[0].astype(BF16))

    q, k, v = _qkv(x2, norm_mix[1][None], at_w_qkv[0].astype(BF16), seq)
    o = _attention(q, k, v, at_sink[0][None], bsz, seq)
    x2 = _proj_mlp(x2, [o], at_w_o[0].astype(BF16), norm_mlp[1][None],
                   mlp_w_up[1].astype(BF16), mlp_w_down[1].astype(BF16), gf=norm_final[None])
    return x2.reshape(bsz, seq, D_MODEL)


def kernel(x_prompt, x_sample, norm_mix, norm_mlp, norm_final, ab_w_in, ab_w_out, cv_dw_w, cv_dw_b,
           cv_ln_g, cv_ln_b, hy_short_w, hy_short_b, hy_w1, hy_b1, hy_w2, hy_b2, hy_w3, hy_b3, hy_w4,
           hy_freq, hy_decay, hy_skip, at_w_qkv, at_sink, at_w_o, mlp_w_up, mlp_w_down):
    weights = (norm_mix, norm_mlp, norm_final, ab_w_in, ab_w_out, cv_dw_w, cv_dw_b, cv_ln_g,
               cv_ln_b, hy_short_w, hy_short_b, hy_w1, hy_b1, hy_w2, hy_b2, hy_w3, hy_b3, hy_w4,
               hy_freq, hy_decay, hy_skip, at_w_qkv, at_sink, at_w_o, mlp_w_up, mlp_w_down)
    return (_trunk(x_prompt, *weights), _trunk(x_sample, *weights))
```

```python
import functools
import math

import numpy as np
import jax
import jax.numpy as jnp
from jax import lax
from jax.experimental import pallas as pl
from jax.experimental.pallas import tpu as pltpu

F32 = jnp.float32
BF16 = jnp.bfloat16

D_MODEL = 1024
CONV_WIDTH = 512
CONV_KERNEL = 31
HYENA_WIDTH = 512
HYENA_IN = 3 * HYENA_WIDTH
HYENA_EMB_DIM = 33
FILTER_HIDDEN = 64
N_HEADS = 16
N_KV_HEADS = 4
HEAD_DIM = 64
ROT_DIM = 16
ROPE_THETA = 500000.0
WINDOW = 128
D_FF = 4 * D_MODEL
NORM_EPS = 1e-5
LN_EPS = 1e-5
FILTER_EPS = 1e-6

LANES = 128
DFT_P = 128
DFT_PH = DFT_P // 2
VMEM_LIMIT = 56 * 1024 * 1024


def _params(*sem):
    return pltpu.CompilerParams(dimension_semantics=sem, vmem_limit_bytes=VMEM_LIMIT)


def _resident(shape):
    nd = len(shape)
    return pl.BlockSpec(shape, lambda *_: (0,) * nd, pipeline_mode=pl.Buffered(1))


def _rms(x, g):
    return x * lax.rsqrt(jnp.mean(x * x, axis=-1, keepdims=True) + NORM_EPS) * g


IN_TM = 512


def _in_proj_kernel(x_ref, g_ref, w_ref, uc_ref, uh_ref):
    hn = _rms(x_ref[...], g_ref[...]).astype(BF16)
    uc_ref[...] = jnp.dot(hn, w_ref[:, :2 * CONV_WIDTH], preferred_element_type=F32)
    uh_ref[...] = jnp.dot(hn, w_ref[:, 2 * CONV_WIDTH:], preferred_element_type=F32)


def _in_proj(x, g, w):
    t = x.shape[0]
    n_out = w.shape[1]
    return pl.pallas_call(
        _in_proj_kernel,
        grid=(t // IN_TM,),
        in_specs=[pl.BlockSpec((IN_TM, D_MODEL), lambda i: (i, 0)),
                  _resident((1, D_MODEL)), _resident((D_MODEL, n_out))],
        out_specs=[pl.BlockSpec((IN_TM, 2 * CONV_WIDTH), lambda i: (i, 0)),
                   pl.BlockSpec((IN_TM, HYENA_IN), lambda i: (i, 0))],
        out_shape=[jax.ShapeDtypeStruct((t, 2 * CONV_WIDTH), F32),
                   jax.ShapeDtypeStruct((t, HYENA_IN), F32)],
        compiler_params=_params("parallel"),
        name="in_proj",
    )(x, g, w)


CC_TR = 512
CC_HALO = 16
CC_RC = 32


def _cconv_kernel(main_ref, prev_ref, next_ref, w_ref, b_ref, g_ref, beta_ref, o_ref, hbuf):
    i = pl.program_id(1)
    last = pl.num_programs(1) - 1

    def glu(u):
        return u[:, :CONV_WIDTH] * jax.nn.sigmoid(u[:, CONV_WIDTH:])

    hbuf[CC_HALO:CC_HALO + CC_TR, :] = glu(main_ref[...])
    hbuf[0:CC_HALO, :] = jnp.where(i > 0, glu(prev_ref[...]), 0.0)
    hbuf[CC_HALO + CC_TR:, :] = jnp.where(i < last, glu(next_ref[...]), 0.0)
    shift = CC_HALO - CONV_KERNEL // 2
    for r in range(CC_TR // CC_RC):
        acc = jnp.broadcast_to(b_ref[...], (CC_RC, CONV_WIDTH))
        for j in range(CONV_KERNEL):
            lo = r * CC_RC + j + shift
            acc = acc + hbuf[lo:lo + CC_RC, :] * w_ref[j:j + 1, :]
        mu = jnp.mean(acc, axis=-1, keepdims=True)
        d = acc - mu
        var = jnp.mean(d * d, axis=-1, keepdims=True)
        y = d * lax.rsqrt(var + LN_EPS) * g_ref[...] + beta_ref[...]
        o_ref[r * CC_RC:(r + 1) * CC_RC, :] = (y * jax.nn.sigmoid(y)).astype(o_ref.dtype)


def _conformer_conv(uc, w, b, g, beta):
    bsz, seq, _ = uc.shape
    nh = CC_TR // CC_HALO
    n_halo = seq // CC_HALO
    return pl.pallas_call(
        _cconv_kernel,
        grid=(bsz, seq // CC_TR),
        in_specs=[
            pl.BlockSpec((None, CC_TR, 2 * CONV_WIDTH), lambda bi, i: (bi, i, 0)),
            pl.BlockSpec((None, CC_HALO, 2 * CONV_WIDTH),
                         lambda bi, i: (bi, jnp.maximum(i * nh - 1, 0), 0)),
            pl.BlockSpec((None, CC_HALO, 2 * CONV_WIDTH),
                         lambda bi, i: (bi, jnp.minimum((i + 1) * nh, n_halo - 1), 0)),
            _resident((CONV_KERNEL, CONV_WIDTH)), _resident((1, CONV_WIDTH)),
            _resident((1, CONV_WIDTH)), _resident((1, CONV_WIDTH)),
        ],
        out_specs=pl.BlockSpec((None, CC_TR, CONV_WIDTH), lambda bi, i: (bi, i, 0)),
        out_shape=jax.ShapeDtypeStruct((bsz, seq, CONV_WIDTH), BF16),
        scratch_shapes=[pltpu.VMEM((CC_TR + 2 * CC_HALO, CONV_WIDTH), F32)],
        compiler_params=_params("parallel", "parallel"),
        name="conformer_conv",
    )(uc, uc, uc, w, b, g, beta)


SC_TR = 512
SC_HALO = 8


def _short_conv_kernel(main_ref, prev_ref, next_ref, w_ref, b_ref, x1_ref, x2_ref, v_ref, buf):
    i = pl.program_id(1)
    last = pl.num_programs(1) - 1
    buf[SC_HALO:SC_HALO + SC_TR, :] = main_ref[...]
    buf[0:SC_HALO, :] = jnp.where(i > 0, prev_ref[...], 0.0)
    buf[SC_HALO + SC_TR:, :] = jnp.where(i < last, next_ref[...], 0.0)
    for n, o_ref in enumerate((x1_ref, x2_ref, v_ref)):
        cs = slice(n * HYENA_WIDTH, (n + 1) * HYENA_WIDTH)
        o_ref[...] = (buf[SC_HALO - 1:SC_HALO - 1 + SC_TR, cs] * w_ref[0:1, cs]
                      + buf[SC_HALO:SC_HALO + SC_TR, cs] * w_ref[1:2, cs]
                      + buf[SC_HALO + 1:SC_HALO + 1 + SC_TR, cs] * w_ref[2:3, cs]
                      + b_ref[:, cs])


def _short_conv(uh, w, b):
    bsz, seq, _ = uh.shape
    nh = SC_TR // SC_HALO
    n_halo = seq // SC_HALO
    out = jax.ShapeDtypeStruct((bsz, seq, HYENA_WIDTH), F32)
    ospec = pl.BlockSpec((None, SC_TR, HYENA_WIDTH), lambda bi, i: (bi, i, 0))
    return pl.pallas_call(
        _short_conv_kernel,
        grid=(bsz, seq // SC_TR),
        in_specs=[
            pl.BlockSpec((None, SC_TR, HYENA_IN), lambda bi, i: (bi, i, 0)),
            pl.BlockSpec((None, SC_HALO, HYENA_IN),
                         lambda bi, i: (bi, jnp.maximum(i * nh - 1, 0), 0)),
            pl.BlockSpec((None, SC_HALO, HYENA_IN),
                         lambda bi, i: (bi, jnp.minimum((i + 1) * nh, n_halo - 1), 0)),
            _resident((3, HYENA_IN)), _resident((1, HYENA_IN)),
        ],
        out_specs=[ospec, ospec, ospec],
        out_shape=[out, out, out],
        scratch_shapes=[pltpu.VMEM((SC_TR + 2 * SC_HALO, HYENA_IN), F32)],
        compiler_params=_params("parallel", "parallel"),
        name="short_conv",
    )(uh, uh, uh, w, b)


FM_TL = 512
FEAT = 64
MASK_COL = HYENA_EMB_DIM
N_FILT = 2 * HYENA_WIDTH


@functools.lru_cache(maxsize=None)
def _filter_features(seq):
    t = np.linspace(0.0, 1.0, seq, dtype=np.float32).astype(np.float64)
    bands = (HYENA_EMB_DIM - 1) // 2
    w = (2.0 * math.pi * np.arange(seq, dtype=np.float32) / np.float32(seq)).astype(np.float64)
    f = np.linspace(1e-4, bands - 1, bands, dtype=np.float32).astype(np.float64)
    fw = w[:, None] * f[None, :]
    z = np.concatenate([t[:, None], np.cos(fw), -np.sin(fw)], axis=-1)
    fwd = np.zeros((seq, FEAT), np.float64)
    fwd[:, :HYENA_EMB_DIM] = z
    fwd[:, MASK_COL] = 1.0
    bwd = np.zeros((seq, FEAT), np.float64)
    bwd[1:, :HYENA_EMB_DIM] = z[:0:-1]
    bwd[1:, MASK_COL] = 1.0
    return np.concatenate([fwd, bwd], axis=-1).astype(np.float32)


def _filter_mlp_kernel(z_ref, w1_ref, b1_ref, w2_ref, b2_ref, w3_ref, b3_ref, w4_ref, fr_ref,
                       dec_ref, k_ref, ssq_ref):
    hp = lax.Precision.HIGHEST
    z = z_ref[...]
    h = jnp.sin(fr_ref[0:1, :] * (jnp.dot(z, w1_ref[...], precision=hp,
                                          preferred_element_type=F32) + b1_ref[...]))
    h = jnp.sin(fr_ref[1:2, :] * (jnp.dot(h, w2_ref[...], precision=hp,
                                          preferred_element_type=F32) + b2_ref[...]))
    h = jnp.sin(fr_ref[2:3, :] * (jnp.dot(h, w3_ref[...], precision=hp,
                                          preferred_element_type=F32) + b3_ref[...]))
    ssq = jnp.zeros((1, N_FILT), F32)
    for d in range(2):
        k = jnp.dot(h, w4_ref[d], precision=hp, preferred_element_type=F32)
        t = z[:, d * FEAT:d * FEAT + 1]
        mask = z[:, d * FEAT + MASK_COL:d * FEAT + MASK_COL + 1]
        k = k * jnp.exp(-t * jnp.abs(dec_ref[d])) * mask
        k_ref[d] = k
        ssq = ssq + jnp.sum(k * k, axis=0, keepdims=True)

    @pl.when(pl.program_id(0) == 0)
    def _():
        ssq_ref[...] = jnp.zeros_like(ssq_ref)

    ssq_ref[...] += ssq


def _filter_mlp(seq, w1, b1, w2, b2, w3, b3, w4, freq, decay):
    fh = FILTER_HIDDEN

    def blockdiag(w):
        z = jnp.zeros_like(w)
        return jnp.concatenate([jnp.concatenate([w, z], 1), jnp.concatenate([z, w], 1)], 0)

    w1p = jnp.zeros((FEAT, fh), F32).at[:HYENA_EMB_DIM].set(w1)
    w4r = w4.reshape(fh, 2, 2, HYENA_WIDTH)
    zeros = jnp.zeros((fh, N_FILT), F32)
    w4f = jnp.concatenate([w4r[:, :, 0].reshape(fh, N_FILT), zeros], 0)
    w4b = jnp.concatenate([zeros, w4r[:, :, 1].reshape(fh, N_FILT)], 0)
    dec = decay.reshape(2, 2, HYENA_WIDTH)
    args = (
        jnp.asarray(_filter_features(seq)),
        blockdiag(w1p), jnp.tile(b1, 2)[None], blockdiag(w2), jnp.tile(b2, 2)[None],
        blockdiag(w3), jnp.tile(b3, 2)[None],
        jnp.stack([w4f, w4b]),
        jnp.tile(freq, (1, 2)),
        jnp.stack([dec[:, 0].reshape(1, N_FILT), dec[:, 1].reshape(1, N_FILT)]),
    )
    in_specs = [pl.BlockSpec((FM_TL, 2 * FEAT), lambda i: (i, 0))]
    in_specs += [_resident(a.shape) for a in args[1:]]
    return pl.pallas_call(
        _filter_mlp_kernel,
        grid=(seq // FM_TL,),
        in_specs=in_specs,
        out_specs=[pl.BlockSpec((2, FM_TL, N_FILT), lambda i: (0, i, 0)),
                   pl.BlockSpec((1, N_FILT), lambda i: (0, 0))],
        out_shape=[jax.ShapeDtypeStruct((2, seq, N_FILT), F32),
                   jax.ShapeDtypeStruct((1, N_FILT), F32)],
        compiler_params=_params("arbitrary"),
        name="filter_mlp",
    )(*args)


@functools.lru_cache(maxsize=None)
def _dft_tables(q):
    n = DFT_P * q
    c = np.arange(DFT_P)[:, None]
    a = np.arange(DFT_P)[None, :]
    b = np.arange(q)[:, None, None]
    ang = -2.0 * np.pi * ((c * (q * a + b)) % n) / n
    fr, fi = np.cos(ang), np.sin(ang)
    d = np.arange(q)[:, None]
    ang2 = -2.0 * np.pi * ((d * np.arange(q)[None, :]) % q) / q
    gr, gi = np.cos(ang2), np.sin(ang2)
    g2 = np.block([[gr, -gi], [gi, gr]])
    g2i = np.block([[gr, gi], [-gi, gr]])
    return (fr.astype(np.float32), fi.astype(np.float32),
            g2.astype(np.float32), g2i.astype(np.float32))


def _dft_mats(q):
    fr, fi, g2, g2i = (jnp.asarray(t) for t in _dft_tables(q))
    frh, fih = fr[:, :, :DFT_PH], fi[:, :, :DFT_PH]
    m1 = jnp.concatenate([jnp.concatenate([frh, -fih], 2), jnp.concatenate([fih, frh], 2)], 1)
    m1f = jnp.concatenate([fr, fi], 1)
    hr, hi = jnp.swapaxes(frh, 1, 2), -jnp.swapaxes(fih, 1, 2)
    m3 = jnp.concatenate([jnp.concatenate([hr, -hi], 2), jnp.concatenate([hi, hr], 2)], 1)
    return (m1.astype(BF16), m1f.astype(BF16), m3.astype(BF16),
            g2.astype(BF16), g2i.astype(BF16))


S1_NB = 8


def _dft_s1_kernel(x_ref, m_ref, ar_ref, ai_ref, *, cw):
    for j in range(S1_NB):
        seg = slice(j * cw, (j + 1) * cw)
        r = jnp.dot(m_ref[j], x_ref[:, seg].astype(BF16), preferred_element_type=F32)
        ar_ref[:, seg] = r[:DFT_P].astype(BF16)
        ai_ref[:, seg] = r[DFT_P:].astype(BF16)


def _dft_stage1(x, m, q, cw):
    g = x.shape[0]
    w = S1_NB * cw
    out = jax.ShapeDtypeStruct((g, DFT_P, q * cw), BF16)
    spec = pl.BlockSpec((None, DFT_P, w), lambda gi, j: (gi, 0, j))
    return pl.pallas_call(
        functools.partial(_dft_s1_kernel, cw=cw),
        grid=(g, q // S1_NB),
        in_specs=[spec, pl.BlockSpec((S1_NB, 2 * DFT_P, DFT_P), lambda gi, j: (j, 0, 0))],
        out_specs=[spec, spec],
        out_shape=[out, out],
        compiler_params=_params("parallel", "parallel"),
        name="dft_stage1",
    )(x, m)


S2_NC = 8


def _filter_s2_kernel(ar_ref, ai_ref, g2_ref, ssq_ref, kr_ref, ki_ref, *, q, inv_n):
    scale = lax.rsqrt(ssq_ref[...] + FILTER_EPS) * inv_n
    for c in range(S2_NC):
        a = jnp.concatenate([ar_ref[c], ai_ref[c]], axis=0)
        x = jnp.dot(g2_ref[...], a, preferred_element_type=F32)
        kr_ref[c] = x[:q] * scale
        ki_ref[c] = x[q:] * scale


def _filter_stage2(ar, ai, g2, ssq, q):
    spec = pl.BlockSpec((S2_NC, q, N_FILT), lambda i: (i, 0, 0))
    out = jax.ShapeDtypeStruct((DFT_P, q, N_FILT), F32)
    return pl.pallas_call(
        functools.partial(_filter_s2_kernel, q=q, inv_n=1.0 / (DFT_P * q)),
        grid=(DFT_P // S2_NC,),
        in_specs=[spec, spec, _resident((2 * q, 2 * q)), _resident((1, N_FILT))],
        out_specs=[spec, spec],
        out_shape=[out, out],
        compiler_params=_params("parallel"),
        name="filter_stage2",
    )(ar, ai, g2, ssq)


def _dft_mid_kernel(ar_ref, ai_ref, g2_ref, g2i_ref, kr_ref, ki_ref, yr_ref, yi_ref, *, q):
    for c in range(S2_NC):
        a = jnp.concatenate([ar_ref[c], ai_ref[c]], axis=0)
        x = jnp.dot(g2_ref[...], a, preferred_element_type=F32)
        xr, xi = x[:q], x[q:]
        kr, ki = kr_ref[c], ki_ref[c]
        z = jnp.concatenate([xr * kr - xi * ki, xr * ki + xi * kr], axis=0).astype(BF16)
        y = jnp.dot(g2i_ref[...], z, preferred_element_type=F32)
        yr_ref[c] = y[:q].astype(BF16)
        yi_ref[c] = y[q:].astype(BF16)


def _dft_mid(ar, ai, g2, g2i, kr, ki, order, q):
    pairs = ar.shape[0]
    cw = HYENA_WIDTH
    spec = pl.BlockSpec((None, S2_NC, q, cw), lambda i, p: (p, i, 0, 0))
    kspec = pl.BlockSpec((S2_NC, q, cw), lambda i, p: (i, 0, order))
    out = jax.ShapeDtypeStruct(ar.shape, BF16)
    return pl.pallas_call(
        functools.partial(_dft_mid_kernel, q=q),
        grid=(DFT_P // S2_NC, pairs),
        in_specs=[spec, spec, _resident((2 * q, 2 * q)), _resident((2 * q, 2 * q)), kspec, kspec],
        out_specs=[spec, spec],
        out_shape=[out, out],
        compiler_params=_params("parallel", "parallel"),
        name="dft_mid",
    )(ar, ai, g2, g2i, kr, ki)


def _dft_s3_kernel(yr_ref, yi_ref, m_ref, gate_ref, z_ref, sk_ref, o_ref, *, cw):
    for j in range(S1_NB):
        seg = slice(j * cw, (j + 1) * cw)
        yc = jnp.concatenate([yr_ref[:, seg], yi_ref[:, seg]], axis=0)
        y = jnp.dot(m_ref[j], yc, preferred_element_type=F32)
        o_ref[:, seg] = (gate_ref[:, seg] * (y + z_ref[:, seg] * sk_ref[...])).astype(o_ref.dtype)


def _dft_stage3(yr, yi, m3, gate, z, sk, q, out_dtype):
    pairs = yr.shape[0]
    cw = HYENA_WIDTH
    w = S1_NB * cw
    spec = pl.BlockSpec((None, DFT_P, w), lambda p, j: (p, 0, j))
    return pl.pallas_call(
        functools.partial(_dft_s3_kernel, cw=cw),
        grid=(pairs, q // S1_NB),
        in_specs=[spec, spec, pl.BlockSpec((S1_NB, DFT_P, 2 * DFT_P), lambda p, j: (j, 0, 0)),
                  spec, spec, _resident((1, cw))],
        out_specs=spec,
        out_shape=jax.ShapeDtypeStruct((pairs, DFT_P, q * cw), out_dtype),
        compiler_params=_params("parallel", "parallel"),
        name="dft_stage3",
    )(yr, yi, m3, gate, z, sk)


def _hyena(uh, short_w, short_b, w1, b1, w2, b2, w3, b3, w4, freq, decay, skip):
    bsz, seq, _ = uh.shape
    pairs = bsz // 2
    q = 2 * seq // DFT_P
    cw = HYENA_WIDTH
    m1, m1f, m3, g2, g2i = _dft_mats(q)

    k_time, ssq = _filter_mlp(seq, w1, b1, w2, b2, w3, b3, w4, freq, decay)
    afr, afi = _dft_stage1(k_time.reshape(1, DFT_P, q * N_FILT), m1f, q, N_FILT)
    kr, ki = _filter_stage2(afr.reshape(DFT_P, q, N_FILT), afi.reshape(DFT_P, q, N_FILT),
                            g2, ssq, q)

    x1, x2, v = _short_conv(uh, short_w, short_b[None])
    pair_view = (pairs, DFT_P, q * cw)
    z = v.reshape(pair_view)
    gates = (x1.reshape(pair_view), x2.reshape(pair_view))
    for n in range(2):
        ar, ai = _dft_stage1(z, m1, q, cw)
        yr, yi = _dft_mid(ar.reshape(pairs, DFT_P, q, cw), ai.reshape(pairs, DFT_P, q, cw),
                          g2, g2i, kr, ki, n, q)
        z = _dft_stage3(yr.reshape(pair_view), yi.reshape(pair_view), m3, gates[n], z,
                        skip[n][None], q, F32 if n == 0 else BF16)
    return z.reshape(bsz, seq, cw)


PM_TM = 512
PM_FC = 1024


def _proj_mlp_kernel(*refs, n_parts, final):
    x_ref = refs[0]
    part_refs = refs[1:1 + n_parts]
    wo_ref, gm_ref, wup_ref, wdn_ref = refs[1 + n_parts:5 + n_parts]
    gf_ref = refs[5 + n_parts] if final else None
    o_ref = refs[-1]
    mixed = jnp.concatenate([p_ref[...] for p_ref in part_refs], axis=1)
    x = x_ref[...] + jnp.dot(mixed, wo_ref[...], preferred_element_type=F32)
    hn = _rms(x, gm_ref[...]).astype(BF16)
    for ch in range(D_FF // PM_FC):
        cs = slice(ch * PM_FC, (ch + 1) * PM_FC)
        h = jnp.dot(hn, wup_ref[:, cs], preferred_element_type=F32)
        h = jnp.square(jnp.maximum(h, 0.0)).astype(BF16)
        x = x + jnp.dot(h, wdn_ref[cs, :], preferred_element_type=F32)
    if final:
        x = _rms(x, gf_ref[...])
    o_ref[...] = x


def _proj_mlp(x, parts, wo, gm, wup, wdn, gf=None):
    t = x.shape[0]
    final = gf is not None
    row = lambda i: (i, 0)
    in_specs = [pl.BlockSpec((PM_TM, D_MODEL), row)]
    in_specs += [pl.BlockSpec((PM_TM, p.shape[1]), row) for p in parts]
    in_specs += [_resident(wo.shape), _resident((1, D_MODEL)), _resident(wup.shape),
                 _resident(wdn.shape)]
    args = [x, *parts, wo, gm, wup, wdn]
    if final:
        in_specs.append(_resident((1, D_MODEL)))
        args.append(gf)
    return pl.pallas_call(
        functools.partial(_proj_mlp_kernel, n_parts=len(parts), final=final),
        grid=(t // PM_TM,),
        in_specs=in_specs,
        out_specs=pl.BlockSpec((PM_TM, D_MODEL), row),
        out_shape=jax.ShapeDtypeStruct((t, D_MODEL), F32),
        compiler_params=_params("parallel"),
        name="proj_mlp",
    )(*args)


QK_TM = 512
Q_WIDTH = N_HEADS * HEAD_DIM
KV_WIDTH = N_KV_HEADS * HEAD_DIM
ROT_HALF = ROT_DIM // 2
LOG2E = math.log2(math.e)
Q_SCALE = HEAD_DIM ** -0.5 * LOG2E


@functools.lru_cache(maxsize=None)
def _rope_tables(seq):
    inv = ROPE_THETA ** (-(np.arange(0, ROT_DIM, 2, dtype=np.float64) / ROT_DIM))
    ang = np.arange(seq, dtype=np.float64)[:, None] * inv[None, :]
    cos = np.ones((seq, HEAD_DIM))
    s_lo = np.zeros((seq, HEAD_DIM))
    s_hi = np.zeros((seq, HEAD_DIM))
    cos[:, :ROT_HALF] = np.cos(ang)
    cos[:, ROT_HALF:ROT_DIM] = np.cos(ang)
    s_lo[:, ROT_HALF:ROT_DIM] = np.sin(ang)
    s_hi[:, :ROT_HALF] = -np.sin(ang)
    rep = LANES // HEAD_DIM
    return np.stack([np.tile(cos, (1, rep)), np.tile(s_lo, (1, rep)),
                     np.tile(s_hi, (1, rep))]).astype(np.float32)


def _qkv_kernel(x_ref, g_ref, w_ref, rope_ref, q_ref, k_ref, v_ref):
    hn = _rms(x_ref[...], g_ref[...]).astype(BF16)
    cos, s_lo, s_hi = rope_ref[0], rope_ref[1], rope_ref[2]
    nq = Q_WIDTH // LANES
    for blk in range((Q_WIDTH + KV_WIDTH) // LANES):
        t = jnp.dot(hn, w_ref[:, blk * LANES:(blk + 1) * LANES], preferred_element_type=F32)
        r = (t * cos + pltpu.roll(t, ROT_HALF, axis=1) * s_lo
             + pltpu.roll(t, LANES - ROT_HALF, axis=1) * s_hi)
        if blk < nq:
            q_ref[:, blk * LANES:(blk + 1) * LANES] = (r * Q_SCALE).astype(BF16)
        else:
            k_ref[:, (blk - nq) * LANES:(blk - nq + 1) * LANES] = r.astype(BF16)
    v_ref[...] = jnp.dot(hn, w_ref[:, Q_WIDTH + KV_WIDTH:],
                         preferred_element_type=F32).astype(BF16)


def _qkv(x, g, w, seq):
    t = x.shape[0]
    per_seq = seq // QK_TM
    row = lambda i: (i, 0)
    return pl.pallas_call(
        _qkv_kernel,
        grid=(t // QK_TM,),
        in_specs=[pl.BlockSpec((QK_TM, D_MODEL), row), _resident((1, D_MODEL)),
                  _resident(w.shape),
                  pl.BlockSpec((3, QK_TM, LANES), lambda i: (0, i % per_seq, 0))],
        out_specs=[pl.BlockSpec((QK_TM, Q_WIDTH), row), pl.BlockSpec((QK_TM, KV_WIDTH), row),
                   pl.BlockSpec((QK_TM, KV_WIDTH), row)],
        out_shape=[jax.ShapeDtypeStruct((t, Q_WIDTH), BF16),
                   jax.ShapeDtypeStruct((t, KV_WIDTH), BF16),
                   jax.ShapeDtypeStruct((t, KV_WIDTH), BF16)],
        compiler_params=_params("parallel"),
        name="qkv_rope",
    )(x, g, w, jnp.asarray(_rope_tables(seq)))


AT_TQ = 512
AT_BLK = WINDOW
AT_KEYS = 3 * AT_BLK
AT_ROWS = AT_TQ + 2 * AT_BLK
GROUP = N_HEADS // N_KV_HEADS


def _attn_kernel(sink_ref, q_ref, k_ref, kp_ref, kn_ref, v_ref, vp_ref, vn_ref, o_ref,
                 klo, khi, vlo, vhi):
    i = pl.program_id(1)
    last = pl.num_programs(1) - 1
    nb = AT_TQ // AT_BLK

    for (prev, main, nxt), lo_ref, hi_ref in (((kp_ref, k_ref, kn_ref), klo, khi),
                                              ((vp_ref, v_ref, vn_ref), vlo, vhi)):
        for row0, ref in ((0, prev), (AT_BLK, main), (AT_BLK + AT_TQ, nxt)):
            nrows = ref.shape[0]
            low = lax.broadcasted_iota(jnp.int32, (nrows, LANES), 1) < HEAD_DIM
            for kvh in range(N_KV_HEADS):
                t = ref[:, (kvh // 2) * LANES:(kvh // 2 + 1) * LANES]
                zero = jnp.zeros_like(t)
                if kvh % 2 == 0:
                    lo = jnp.where(low, t, zero)
                    hi = jnp.concatenate([zero[:, :HEAD_DIM], t[:, :HEAD_DIM]], axis=1)
                else:
                    hi = jnp.where(low, zero, t)
                    lo = jnp.concatenate([t[:, HEAD_DIM:], zero[:, :HEAD_DIM]], axis=1)
                lo_ref[kvh, row0:row0 + nrows, :] = lo
                hi_ref[kvh, row0:row0 + nrows, :] = hi

    rows2 = 2 * AT_BLK
    qrow = lax.broadcasted_iota(jnp.int32, (rows2, AT_BLK), 0) & (AT_BLK - 1)
    kcol = lax.broadcasted_iota(jnp.int32, (rows2, AT_BLK), 1)
    upper = lax.broadcasted_iota(jnp.int32, (rows2, 1), 0) < AT_BLK
    low_lane = lax.broadcasted_iota(jnp.int32, (rows2, LANES), 1) < HEAD_DIM

    def body(jb, carry):
        r0 = pl.multiple_of(jb * AT_BLK, AT_BLK)
        off_prev = jnp.where((i == 0) & (jb == 0), AT_BLK, 0)
        off_next = jnp.where((i == last) & (jb == nb - 1), AT_BLK, 0)
        m_prev = kcol >= qrow + off_prev
        m_next = kcol <= qrow - off_next
        for kvh in range(N_KV_HEADS):
            col = kvh * GROUP * HEAD_DIM
            qs = jnp.concatenate([q_ref[pl.ds(r0, AT_BLK), col:col + LANES],
                                  q_ref[pl.ds(r0, AT_BLK), col + LANES:col + 2 * LANES]], axis=0)
            kc = jnp.concatenate([klo[kvh, pl.ds(r0, AT_KEYS), :],
                                  khi[kvh, pl.ds(r0, AT_KEYS), :]], axis=0)
            vc = jnp.concatenate([vlo[kvh, pl.ds(r0, AT_KEYS), :],
                                  vhi[kvh, pl.ds(r0, AT_KEYS), :]], axis=0)
            s = lax.dot_general(qs, kc, (((1,), (1,)), ((), ())),
                                preferred_element_type=F32)
            probs, rinv = [], []
            for half in range(2):
                head = kvh * GROUP + half
                sink = jnp.where(upper, sink_ref[head] * LOG2E, sink_ref[head + 2] * LOG2E)
                c0 = half * AT_KEYS
                a = jnp.where(m_prev, s[:, c0:c0 + AT_BLK], -jnp.inf)
                b = s[:, c0 + AT_BLK:c0 + 2 * AT_BLK]
                c = jnp.where(m_next, s[:, c0 + 2 * AT_BLK:c0 + 3 * AT_BLK], -jnp.inf)
                m = jnp.max(jnp.maximum(jnp.maximum(a, b), c), axis=-1, keepdims=True)
                m = jnp.maximum(m, sink)
                pa, pb, pc = jnp.exp2(a - m), jnp.exp2(b - m), jnp.exp2(c - m)
                denom = jnp.sum(pa + pb + pc, axis=-1, keepdims=True) + jnp.exp2(sink - m)
                probs += [pa, pb, pc]
                rinv.append(1.0 / denom)
            p = jnp.concatenate(probs, axis=1).astype(BF16)
            o = jnp.dot(p, vc, preferred_element_type=F32)
            o = (o * jnp.where(low_lane, rinv[0], rinv[1])).astype(BF16)
            o_ref[pl.ds(r0, AT_BLK), col:col + LANES] = o[:AT_BLK]
            o_ref[pl.ds(r0, AT_BLK), col + LANES:col + 2 * LANES] = o[AT_BLK:]
        return carry

    lax.fori_loop(0, nb, body, 0)


def _attention(q, k, v, sink, bsz, seq):
    nb = AT_TQ // AT_BLK
    n_blk = seq // AT_BLK
    main = lambda w: pl.BlockSpec((None, AT_TQ, w), lambda b, i: (b, i, 0))
    prev = pl.BlockSpec((None, AT_BLK, KV_WIDTH), lambda b, i: (b, jnp.maximum(i * nb - 1, 0), 0))
    nxt = pl.BlockSpec((None, AT_BLK, KV_WIDTH),
                       lambda b, i: (b, jnp.minimum((i + 1) * nb, n_blk - 1), 0))
    q3 = q.reshape(bsz, seq, Q_WIDTH)
    k3 = k.reshape(bsz, seq, KV_WIDTH)
    v3 = v.reshape(bsz, seq, KV_WIDTH)
    head_copy = pltpu.VMEM((N_KV_HEADS, AT_ROWS, LANES), BF16)
    out = pl.pallas_call(
        _attn_kernel,
        grid=(bsz, seq // AT_TQ),
        in_specs=[pl.BlockSpec(memory_space=pltpu.SMEM),
                  main(Q_WIDTH), main(KV_WIDTH), prev, nxt, main(KV_WIDTH), prev, nxt],
        out_specs=main(Q_WIDTH),
        out_shape=jax.ShapeDtypeStruct((bsz, seq, Q_WIDTH), BF16),
        scratch_shapes=[head_copy, head_copy, head_copy, head_copy],
        compiler_params=_params("parallel", "parallel"),
        name="window_attention",
    )(sink, q3, k3, k3, k3, v3, v3, v3)
    return out.reshape(bsz * seq, Q_WIDTH)


def _trunk(x, norm_mix, norm_mlp, norm_final, ab_w_in, ab_w_out, cv_dw_w, cv_dw_b, cv_ln_g,
           cv_ln_b, hy_short_w, hy_short_b, hy_w1, hy_b1, hy_w2, hy_b2, hy_w3, hy_b3, hy_w4,
           hy_freq, hy_decay, hy_skip, at_w_qkv, at_sink, at_w_o, mlp_w_up, mlp_w_down):
    bsz, seq, _ = x.shape
    t = bsz * seq
    x2 = x.reshape(t, D_MODEL)

    uc, uh = _in_proj(x2, norm_mix[0][None], ab_w_in[0].astype(BF16))
    y_a = _conformer_conv(uc.reshape(bsz, seq, 2 * CONV_WIDTH), cv_dw_w[0], cv_dw_b[0][None],
                          cv_ln_g[0][None], cv_ln_b[0][None])
    y_b = _hyena(uh.reshape(bsz, seq, HYENA_IN), hy_short_w[0], hy_short_b[0], hy_w1[0], hy_b1[0],
                 hy_w2[0], hy_b2[0], hy_w3[0], hy_b3[0], hy_w4[0], hy_freq[0], hy_decay[0],
                 hy_skip[0])
    x2 = _proj_mlp(x2, [y_a.reshape(t, CONV_WIDTH), y_b.reshape(t, HYENA_WIDTH)],
                   ab_w_out[0].astype(BF16), norm_mlp[0][None], mlp_w_up[0].astype(BF16),
                   mlp_w_down[0].astype(BF16))

    q, k, v = _qkv(x2, norm_mix[1][None], at_w_qkv[0].astype(BF16), seq)
    o = _attention(q, k, v, at_sink[0], bsz, seq)
    x2 = _proj_mlp(x2, [o], at_w_o[0].astype(BF16), norm_mlp[1][None],
                   mlp_w_up[1].astype(BF16), mlp_w_down[1].astype(BF16), gf=norm_final[None])
    return x2.reshape(bsz, seq, D_MODEL)


def kernel(x_prompt, x_sample, norm_mix, norm_mlp, norm_final, ab_w_in, ab_w_out, cv_dw_w, cv_dw_b,
           cv_ln_g, cv_ln_b, hy_short_w, hy_short_b, hy_w1, hy_b1, hy_w2, hy_b2, hy_w3, hy_b3, hy_w4,
           hy_freq, hy_decay, hy_skip, at_w_qkv, at_sink, at_w_o, mlp_w_up, mlp_w_down):
    weights = (norm_mix, norm_mlp, norm_final, ab_w_in, ab_w_out, cv_dw_w, cv_dw_b, cv_ln_g,
               cv_ln_b, hy_short_w, hy_short_b, hy_w1, hy_b1, hy_w2, hy_b2, hy_w3, hy_b3, hy_w4,
               hy_freq, hy_decay, hy_skip, at_w_qkv, at_sink, at_w_o, mlp_w_up, mlp_w_down)
    return (_trunk(x_prompt, *weights), _trunk(x_sample, *weights))
```

```python
import functools
import math

import numpy as np
import jax
import jax.numpy as jnp
from jax import lax
from jax.experimental import pallas as pl
from jax.experimental.pallas import tpu as pltpu

F32 = jnp.float32
BF16 = jnp.bfloat16

D_MODEL = 1024
CONV_WIDTH = 512
CONV_KERNEL = 31
HYENA_WIDTH = 512
HYENA_IN = 3 * HYENA_WIDTH
HYENA_EMB_DIM = 33
FILTER_HIDDEN = 64
N_HEADS = 16
N_KV_HEADS = 4
HEAD_DIM = 64
ROT_DIM = 16
ROPE_THETA = 500000.0
WINDOW = 128
D_FF = 4 * D_MODEL
NORM_EPS = 1e-5
LN_EPS = 1e-5
FILTER_EPS = 1e-6

LANES = 128
DFT_P = 128
DFT_PH = DFT_P // 2
VMEM_LIMIT = 56 * 1024 * 1024


def _params(*sem):
    return pltpu.CompilerParams(dimension_semantics=sem, vmem_limit_bytes=VMEM_LIMIT)


def _resident(shape):
    nd = len(shape)
    return pl.BlockSpec(shape, lambda *_: (0,) * nd, pipeline_mode=pl.Buffered(1))


def _rms(x, g):
    return x * lax.rsqrt(jnp.mean(x * x, axis=-1, keepdims=True) + NORM_EPS) * g


IN_TM = 512


def _in_proj_kernel(x_ref, g_ref, w_ref, uc_ref, uh_ref):
    hn = _rms(x_ref[...], g_ref[...]).astype(BF16)
    uc_ref[...] = jnp.dot(hn, w_ref[:, :2 * CONV_WIDTH], preferred_element_type=F32)
    uh_ref[...] = jnp.dot(hn, w_ref[:, 2 * CONV_WIDTH:], preferred_element_type=F32)


def _in_proj(x, g, w):
    t = x.shape[0]
    n_out = w.shape[1]
    return pl.pallas_call(
        _in_proj_kernel,
        grid=(t // IN_TM,),
        in_specs=[pl.BlockSpec((IN_TM, D_MODEL), lambda i: (i, 0)),
                  _resident((1, D_MODEL)), _resident((D_MODEL, n_out))],
        out_specs=[pl.BlockSpec((IN_TM, 2 * CONV_WIDTH), lambda i: (i, 0)),
                   pl.BlockSpec((IN_TM, HYENA_IN), lambda i: (i, 0))],
        out_shape=[jax.ShapeDtypeStruct((t, 2 * CONV_WIDTH), F32),
                   jax.ShapeDtypeStruct((t, HYENA_IN), F32)],
        compiler_params=_params("parallel"),
        name="in_proj",
    )(x, g, w)


CC_TR = 512
CC_HALO = 16
CC_RC = 32
SUBLANES = 8
CC_SHIFT0 = CC_HALO - CONV_KERNEL // 2
CC_MAX_ALIGNED = (CC_SHIFT0 + CONV_KERNEL - 1) // SUBLANES * SUBLANES
CC_SH_ROWS = CC_TR + CC_MAX_ALIGNED


def _cconv_kernel(main_ref, prev_ref, next_ref, w_ref, b_ref, g_ref, beta_ref, o_ref, hbuf, hsh):
    i = pl.program_id(1)
    last = pl.num_programs(1) - 1

    def glu(u):
        return u[:, :CONV_WIDTH] * jax.nn.sigmoid(u[:, CONV_WIDTH:])

    hbuf[CC_HALO:CC_HALO + CC_TR, :] = glu(main_ref[...])
    hbuf[0:CC_HALO, :] = jnp.where(i > 0, glu(prev_ref[...]), 0.0)
    hbuf[CC_HALO + CC_TR:, :] = jnp.where(i < last, glu(next_ref[...]), 0.0)
    for ph in range(SUBLANES):
        hsh[ph] = hbuf[ph:ph + CC_SH_ROWS, :]
    for r in range(CC_TR // CC_RC):
        acc = jnp.broadcast_to(b_ref[...], (CC_RC, CONV_WIDTH))
        for j in range(CONV_KERNEL):
            ph, lo = (j + CC_SHIFT0) % SUBLANES, r * CC_RC + (j + CC_SHIFT0) // SUBLANES * SUBLANES
            acc = acc + hsh[ph, lo:lo + CC_RC, :] * w_ref[j:j + 1, :]
        mu = jnp.mean(acc, axis=-1, keepdims=True)
        d = acc - mu
        var = jnp.mean(d * d, axis=-1, keepdims=True)
        y = d * lax.rsqrt(var + LN_EPS) * g_ref[...] + beta_ref[...]
        o_ref[r * CC_RC:(r + 1) * CC_RC, :] = (y * jax.nn.sigmoid(y)).astype(o_ref.dtype)


def _conformer_conv(uc, w, b, g, beta):
    bsz, seq, _ = uc.shape
    nh = CC_TR // CC_HALO
    n_halo = seq // CC_HALO
    return pl.pallas_call(
        _cconv_kernel,
        grid=(bsz, seq // CC_TR),
        in_specs=[
            pl.BlockSpec((None, CC_TR, 2 * CONV_WIDTH), lambda bi, i: (bi, i, 0)),
            pl.BlockSpec((None, CC_HALO, 2 * CONV_WIDTH),
                         lambda bi, i: (bi, jnp.maximum(i * nh - 1, 0), 0)),
            pl.BlockSpec((None, CC_HALO, 2 * CONV_WIDTH),
                         lambda bi, i: (bi, jnp.minimum((i + 1) * nh, n_halo - 1), 0)),
            _resident((CONV_KERNEL, CONV_WIDTH)), _resident((1, CONV_WIDTH)),
            _resident((1, CONV_WIDTH)), _resident((1, CONV_WIDTH)),
        ],
        out_specs=pl.BlockSpec((None, CC_TR, CONV_WIDTH), lambda bi, i: (bi, i, 0)),
        out_shape=jax.ShapeDtypeStruct((bsz, seq, CONV_WIDTH), BF16),
        scratch_shapes=[pltpu.VMEM((CC_TR + 2 * CC_HALO, CONV_WIDTH), F32),
                        pltpu.VMEM((SUBLANES, CC_SH_ROWS, CONV_WIDTH), F32)],
        compiler_params=_params("parallel", "parallel"),
        name="conformer_conv",
    )(uc, uc, uc, w, b, g, beta)


SC_TR = 512
SC_HALO = 8


def _short_conv_kernel(main_ref, prev_ref, next_ref, w_ref, b_ref, x1_ref, x2_ref, v_ref, buf):
    i = pl.program_id(1)
    last = pl.num_programs(1) - 1
    buf[SC_HALO:SC_HALO + SC_TR, :] = main_ref[...]
    buf[0:SC_HALO, :] = jnp.where(i > 0, prev_ref[...], 0.0)
    buf[SC_HALO + SC_TR:, :] = jnp.where(i < last, next_ref[...], 0.0)
    for n, o_ref in enumerate((x1_ref, x2_ref, v_ref)):
        cs = slice(n * HYENA_WIDTH, (n + 1) * HYENA_WIDTH)
        o_ref[...] = (buf[SC_HALO - 1:SC_HALO - 1 + SC_TR, cs] * w_ref[0:1, cs]
                      + buf[SC_HALO:SC_HALO + SC_TR, cs] * w_ref[1:2, cs]
                      + buf[SC_HALO + 1:SC_HALO + 1 + SC_TR, cs] * w_ref[2:3, cs]
                      + b_ref[:, cs])


def _short_conv(uh, w, b):
    bsz, seq, _ = uh.shape
    nh = SC_TR // SC_HALO
    n_halo = seq // SC_HALO
    out = jax.ShapeDtypeStruct((bsz, seq, HYENA_WIDTH), F32)
    ospec = pl.BlockSpec((None, SC_TR, HYENA_WIDTH), lambda bi, i: (bi, i, 0))
    return pl.pallas_call(
        _short_conv_kernel,
        grid=(bsz, seq // SC_TR),
        in_specs=[
            pl.BlockSpec((None, SC_TR, HYENA_IN), lambda bi, i: (bi, i, 0)),
            pl.BlockSpec((None, SC_HALO, HYENA_IN),
                         lambda bi, i: (bi, jnp.maximum(i * nh - 1, 0), 0)),
            pl.BlockSpec((None, SC_HALO, HYENA_IN),
                         lambda bi, i: (bi, jnp.minimum((i + 1) * nh, n_halo - 1), 0)),
            _resident((3, HYENA_IN)), _resident((1, HYENA_IN)),
        ],
        out_specs=[ospec, ospec, ospec],
        out_shape=[out, out, out],
        scratch_shapes=[pltpu.VMEM((SC_TR + 2 * SC_HALO, HYENA_IN), F32)],
        compiler_params=_params("parallel", "parallel"),
        name="short_conv",
    )(uh, uh, uh, w, b)


FM_TL = 512
FEAT = 64
MASK_COL = HYENA_EMB_DIM
N_FILT = 2 * HYENA_WIDTH


@functools.lru_cache(maxsize=None)
def _filter_features(seq):
    t = np.linspace(0.0, 1.0, seq, dtype=np.float32).astype(np.float64)
    bands = (HYENA_EMB_DIM - 1) // 2
    w = (2.0 * math.pi * np.arange(seq, dtype=np.float32) / np.float32(seq)).astype(np.float64)
    f = np.linspace(1e-4, bands - 1, bands, dtype=np.float32).astype(np.float64)
    fw = w[:, None] * f[None, :]
    z = np.concatenate([t[:, None], np.cos(fw), -np.sin(fw)], axis=-1)
    fwd = np.zeros((seq, FEAT), np.float64)
    fwd[:, :HYENA_EMB_DIM] = z
    fwd[:, MASK_COL] = 1.0
    bwd = np.zeros((seq, FEAT), np.float64)
    bwd[1:, :HYENA_EMB_DIM] = z[:0:-1]
    bwd[1:, MASK_COL] = 1.0
    return np.concatenate([fwd, bwd], axis=-1).astype(np.float32)


def _filter_mlp_kernel(z_ref, w1_ref, b1_ref, w2_ref, b2_ref, w3_ref, b3_ref, w4_ref, fr_ref,
                       dec_ref, k_ref, ssq_ref):
    hp = lax.Precision.HIGHEST
    z = z_ref[...]
    h = jnp.sin(fr_ref[0:1, :] * (jnp.dot(z, w1_ref[...], precision=hp,
                                          preferred_element_type=F32) + b1_ref[...]))
    h = jnp.sin(fr_ref[1:2, :] * (jnp.dot(h, w2_ref[...], precision=hp,
                                          preferred_element_type=F32) + b2_ref[...]))
    h = jnp.sin(fr_ref[2:3, :] * (jnp.dot(h, w3_ref[...], precision=hp,
                                          preferred_element_type=F32) + b3_ref[...]))
    ssq = jnp.zeros((1, N_FILT), F32)
    for d in range(2):
        k = jnp.dot(h, w4_ref[d], precision=hp, preferred_element_type=F32)
        t = z[:, d * FEAT:d * FEAT + 1]
        mask = z[:, d * FEAT + MASK_COL:d * FEAT + MASK_COL + 1]
        k = k * jnp.exp(-t * jnp.abs(dec_ref[d])) * mask
        k_ref[d] = k
        ssq = ssq + jnp.sum(k * k, axis=0, keepdims=True)

    @pl.when(pl.program_id(0) == 0)
    def _():
        ssq_ref[...] = jnp.zeros_like(ssq_ref)

    ssq_ref[...] += ssq


def _filter_mlp(seq, w1, b1, w2, b2, w3, b3, w4, freq, decay):
    fh = FILTER_HIDDEN

    def blockdiag(w):
        z = jnp.zeros_like(w)
        return jnp.concatenate([jnp.concatenate([w, z], 1), jnp.concatenate([z, w], 1)], 0)

    w1p = jnp.zeros((FEAT, fh), F32).at[:HYENA_EMB_DIM].set(w1)
    w4r = w4.reshape(fh, 2, 2, HYENA_WIDTH)
    zeros = jnp.zeros((fh, N_FILT), F32)
    w4f = jnp.concatenate([w4r[:, :, 0].reshape(fh, N_FILT), zeros], 0)
    w4b = jnp.concatenate([zeros, w4r[:, :, 1].reshape(fh, N_FILT)], 0)
    dec = decay.reshape(2, 2, HYENA_WIDTH)
    args = (
        jnp.asarray(_filter_features(seq)),
        blockdiag(w1p), jnp.tile(b1, 2)[None], blockdiag(w2), jnp.tile(b2, 2)[None],
        blockdiag(w3), jnp.tile(b3, 2)[None],
        jnp.stack([w4f, w4b]),
        jnp.tile(freq, (1, 2)),
        jnp.stack([dec[:, 0].reshape(1, N_FILT), dec[:, 1].reshape(1, N_FILT)]),
    )
    in_specs = [pl.BlockSpec((FM_TL, 2 * FEAT), lambda i: (i, 0))]
    in_specs += [_resident(a.shape) for a in args[1:]]
    return pl.pallas_call(
        _filter_mlp_kernel,
        grid=(seq // FM_TL,),
        in_specs=in_specs,
        out_specs=[pl.BlockSpec((2, FM_TL, N_FILT), lambda i: (0, i, 0)),
                   pl.BlockSpec((1, N_FILT), lambda i: (0, 0))],
        out_shape=[jax.ShapeDtypeStruct((2, seq, N_FILT), F32),
                   jax.ShapeDtypeStruct((1, N_FILT), F32)],
        compiler_params=_params("arbitrary"),
        name="filter_mlp",
    )(*args)


@functools.lru_cache(maxsize=None)
def _dft_tables(q):
    n = DFT_P * q
    c = np.arange(DFT_P)[:, None]
    a = np.arange(DFT_P)[None, :]
    b = np.arange(q)[:, None, None]
    ang = -2.0 * np.pi * ((c * (q * a + b)) % n) / n
    fr, fi = np.cos(ang), np.sin(ang)
    d = np.arange(q)[:, None]
    ang2 = -2.0 * np.pi * ((d * np.arange(q)[None, :]) % q) / q
    gr, gi = np.cos(ang2), np.sin(ang2)
    g2 = np.block([[gr, -gi], [gi, gr]])
    g2i = np.block([[gr, gi], [-gi, gr]])
    return (fr.astype(np.float32), fi.astype(np.float32),
            g2.astype(np.float32), g2i.astype(np.float32))


def _dft_mats(q):
    fr, fi, g2, g2i = (jnp.asarray(t) for t in _dft_tables(q))
    frh, fih = fr[:, :, :DFT_PH], fi[:, :, :DFT_PH]
    m1 = jnp.concatenate([jnp.concatenate([frh, -fih], 2), jnp.concatenate([fih, frh], 2)], 1)
    m1f = jnp.concatenate([fr, fi], 1)
    hr, hi = jnp.swapaxes(frh, 1, 2), -jnp.swapaxes(fih, 1, 2)
    m3 = jnp.concatenate([jnp.concatenate([hr, -hi], 2), jnp.concatenate([hi, hr], 2)], 1)
    return (m1.astype(BF16), m1f.astype(BF16), m3.astype(BF16),
            g2.astype(BF16), g2i.astype(BF16))


S1_NB = 16
S1_CW = 256


def _dft_s1_kernel(x_ref, m_ref, ar_ref, ai_ref, re_buf, im_buf):
    x = x_ref[...].astype(BF16).reshape(DFT_P, S1_NB, S1_CW)
    xs = jnp.swapaxes(x, 0, 1)
    for j in range(S1_NB):
        r = jnp.dot(m_ref[j], xs[j], preferred_element_type=F32)
        re_buf[j] = r[:DFT_P].astype(BF16)
        im_buf[j] = r[DFT_P:].astype(BF16)
    ar_ref[...] = jnp.swapaxes(re_buf[...], 0, 1)
    ai_ref[...] = jnp.swapaxes(im_buf[...], 0, 1)


def _dft_stage1(x, m, q):
    g, wtot = x.shape[0] // 2, x.shape[3]
    out = jax.ShapeDtypeStruct((g, DFT_P, q, wtot), BF16)
    ospec = pl.BlockSpec((None, DFT_P, S1_NB, S1_CW), lambda gi, j, ci: (gi, 0, j, ci))
    stage = pltpu.VMEM((S1_NB, DFT_P, S1_CW), BF16)
    return pl.pallas_call(
        _dft_s1_kernel,
        grid=(g, q // S1_NB, wtot // S1_CW),
        in_specs=[pl.BlockSpec((2, DFT_PH, S1_NB, S1_CW), lambda gi, j, ci: (gi, 0, j, ci)),
                  pl.BlockSpec((S1_NB, 2 * DFT_P, DFT_P), lambda gi, j, ci: (j, 0, 0))],
        out_specs=[ospec, ospec],
        out_shape=[out, out],
        scratch_shapes=[stage, stage],
        compiler_params=_params("parallel", "parallel", "parallel"),
        name="dft_stage1",
    )(x, m)


S2_NC = 8


def _filter_s2_kernel(ar_ref, ai_ref, g2_ref, ssq_ref, kr_ref, ki_ref, *, q, inv_n):
    scale = lax.rsqrt(ssq_ref[...] + FILTER_EPS) * inv_n
    for c in range(S2_NC):
        a = jnp.concatenate([ar_ref[c], ai_ref[c]], axis=0)
        x = jnp.dot(g2_ref[...], a, preferred_element_type=F32)
        kr_ref[c] = x[:q] * scale
        ki_ref[c] = x[q:] * scale


def _filter_stage2(ar, ai, g2, ssq, q):
    spec = pl.BlockSpec((S2_NC, q, N_FILT), lambda i: (i, 0, 0))
    out = jax.ShapeDtypeStruct((DFT_P, q, N_FILT), F32)
    return pl.pallas_call(
        functools.partial(_filter_s2_kernel, q=q, inv_n=1.0 / (DFT_P * q)),
        grid=(DFT_P // S2_NC,),
        in_specs=[spec, spec, _resident((2 * q, 2 * q)), _resident((1, N_FILT))],
        out_specs=[spec, spec],
        out_shape=[out, out],
        compiler_params=_params("parallel"),
        name="filter_stage2",
    )(ar, ai, g2, ssq)


def _dft_mid_kernel(ar_ref, ai_ref, g2_ref, g2i_ref, kr_ref, ki_ref, yr_ref, yi_ref, *, q):
    for c in range(S2_NC):
        a = jnp.concatenate([ar_ref[c], ai_ref[c]], axis=0)
        x = jnp.dot(g2_ref[...], a, preferred_element_type=F32)
        xr, xi = x[:q], x[q:]
        kr, ki = kr_ref[c], ki_ref[c]
        z = jnp.concatenate([xr * kr - xi * ki, xr * ki + xi * kr], axis=0).astype(BF16)
        y = jnp.dot(g2i_ref[...], z, preferred_element_type=F32)
        yr_ref[c] = y[:q].astype(BF16)
        yi_ref[c] = y[q:].astype(BF16)


def _dft_mid(ar, ai, g2, g2i, kr, ki, order, q):
    pairs = ar.shape[0]
    cw = HYENA_WIDTH
    spec = pl.BlockSpec((None, S2_NC, q, cw), lambda i, p: (p, i, 0, 0))
    kspec = pl.BlockSpec((S2_NC, q, cw), lambda i, p: (i, 0, order))
    out = jax.ShapeDtypeStruct(ar.shape, BF16)
    return pl.pallas_call(
        functools.partial(_dft_mid_kernel, q=q),
        grid=(DFT_P // S2_NC, pairs),
        in_specs=[spec, spec, _resident((2 * q, 2 * q)), _resident((2 * q, 2 * q)), kspec, kspec],
        out_specs=[spec, spec],
        out_shape=[out, out],
        compiler_params=_params("parallel", "parallel"),
        name="dft_mid",
    )(ar, ai, g2, g2i, kr, ki)


def _dft_s3_kernel(yr_ref, yi_ref, m_ref, gate_ref, z_ref, sk_ref, o_ref, y_buf):
    yr = jnp.swapaxes(yr_ref[...], 0, 1)
    yi = jnp.swapaxes(yi_ref[...], 0, 1)
    for j in range(S1_NB):
        yc = jnp.concatenate([yr[j], yi[j]], axis=0)
        y_buf[j] = jnp.dot(m_ref[j], yc, preferred_element_type=F32)
    y = jnp.swapaxes(y_buf[...], 0, 1).reshape(2, DFT_PH, S1_NB, S1_CW)
    o_ref[...] = (gate_ref[...] * (y + z_ref[...] * sk_ref[...])).astype(o_ref.dtype)


def _dft_stage3(yr, yi, m3, gate, z, sk, q, out_dtype):
    pairs, cw = yr.shape[0], yr.shape[3]
    yspec = pl.BlockSpec((None, DFT_P, S1_NB, S1_CW), lambda p, j, ci: (p, 0, j, ci))
    tspec = pl.BlockSpec((2, DFT_PH, S1_NB, S1_CW), lambda p, j, ci: (p, 0, j, ci))
    return pl.pallas_call(
        _dft_s3_kernel,
        grid=(pairs, q // S1_NB, cw // S1_CW),
        in_specs=[yspec, yspec,
                  pl.BlockSpec((S1_NB, DFT_P, 2 * DFT_P), lambda p, j, ci: (j, 0, 0)),
                  tspec, tspec, pl.BlockSpec((1, S1_CW), lambda p, j, ci: (0, ci))],
        out_specs=tspec,
        out_shape=jax.ShapeDtypeStruct(z.shape, out_dtype),
        scratch_shapes=[pltpu.VMEM((S1_NB, DFT_P, S1_CW), F32)],
        compiler_params=_params("parallel", "parallel", "parallel"),
        name="dft_stage3",
    )(yr, yi, m3, gate, z, sk)


def _hyena(uh, short_w, short_b, w1, b1, w2, b2, w3, b3, w4, freq, decay, skip):
    bsz, seq, _ = uh.shape
    q = 2 * seq // DFT_P
    cw = HYENA_WIDTH
    m1, m1f, m3, g2, g2i = _dft_mats(q)

    k_time, ssq = _filter_mlp(seq, w1, b1, w2, b2, w3, b3, w4, freq, decay)
    afr, afi = _dft_stage1(k_time.reshape(2, DFT_PH, q, N_FILT), m1f, q)
    kr, ki = _filter_stage2(afr.reshape(DFT_P, q, N_FILT), afi.reshape(DFT_P, q, N_FILT),
                            g2, ssq, q)

    x1, x2, v = _short_conv(uh, short_w, short_b[None])
    time_view = (bsz, DFT_PH, q, cw)
    z = v.reshape(time_view)
    gates = (x1.reshape(time_view), x2.reshape(time_view))
    for n in range(2):
        ar, ai = _dft_stage1(z, m1, q)
        yr, yi = _dft_mid(ar, ai, g2, g2i, kr, ki, n, q)
        z = _dft_stage3(yr, yi, m3, gates[n], z, skip[n][None], q, F32 if n == 0 else BF16)
    return z.reshape(bsz, seq, cw)


PM_TM = 512
PM_FC = 1024


def _proj_mlp_kernel(*refs, n_parts, final):
    x_ref = refs[0]
    part_refs = refs[1:1 + n_parts]
    wo_ref, gm_ref, wup_ref, wdn_ref = refs[1 + n_parts:5 + n_parts]
    gf_ref = refs[5 + n_parts] if final else None
    o_ref = refs[-1]
    mixed = jnp.concatenate([p_ref[...] for p_ref in part_refs], axis=1)
    x = x_ref[...] + jnp.dot(mixed, wo_ref[...], preferred_element_type=F32)
    hn = _rms(x, gm_ref[...]).astype(BF16)
    for ch in range(D_FF // PM_FC):
        cs = slice(ch * PM_FC, (ch + 1) * PM_FC)
        h = jnp.dot(hn, wup_ref[:, cs], preferred_element_type=F32)
        h = jnp.square(jnp.maximum(h, 0.0)).astype(BF16)
        x = x + jnp.dot(h, wdn_ref[cs, :], preferred_element_type=F32)
    if final:
        x = _rms(x, gf_ref[...])
    o_ref[...] = x


def _proj_mlp(x, parts, wo, gm, wup, wdn, gf=None):
    t = x.shape[0]
    final = gf is not None
    row = lambda i: (i, 0)
    in_specs = [pl.BlockSpec((PM_TM, D_MODEL), row)]
    in_specs += [pl.BlockSpec((PM_TM, p.shape[1]), row) for p in parts]
    in_specs += [_resident(wo.shape), _resident((1, D_MODEL)), _resident(wup.shape),
                 _resident(wdn.shape)]
    args = [x, *parts, wo, gm, wup, wdn]
    if final:
        in_specs.append(_resident((1, D_MODEL)))
        args.append(gf)
    return pl.pallas_call(
        functools.partial(_proj_mlp_kernel, n_parts=len(parts), final=final),
        grid=(t // PM_TM,),
        in_specs=in_specs,
        out_specs=pl.BlockSpec((PM_TM, D_MODEL), row),
        out_shape=jax.ShapeDtypeStruct((t, D_MODEL), F32),
        compiler_params=_params("parallel"),
        name="proj_mlp",
    )(*args)


QK_TM = 512
Q_WIDTH = N_HEADS * HEAD_DIM
KV_WIDTH = N_KV_HEADS * HEAD_DIM
ROT_HALF = ROT_DIM // 2
LOG2E = math.log2(math.e)
Q_SCALE = HEAD_DIM ** -0.5 * LOG2E


@functools.lru_cache(maxsize=None)
def _rope_tables(seq):
    inv = ROPE_THETA ** (-(np.arange(0, ROT_DIM, 2, dtype=np.float64) / ROT_DIM))
    ang = np.arange(seq, dtype=np.float64)[:, None] * inv[None, :]
    cos = np.ones((seq, HEAD_DIM))
    s_lo = np.zeros((seq, HEAD_DIM))
    s_hi = np.zeros((seq, HEAD_DIM))
    cos[:, :ROT_HALF] = np.cos(ang)
    cos[:, ROT_HALF:ROT_DIM] = np.cos(ang)
    s_lo[:, ROT_HALF:ROT_DIM] = np.sin(ang)
    s_hi[:, :ROT_HALF] = -np.sin(ang)
    rep = LANES // HEAD_DIM
    return np.stack([np.tile(cos, (1, rep)), np.tile(s_lo, (1, rep)),
                     np.tile(s_hi, (1, rep))]).astype(np.float32)


def _qkv_kernel(x_ref, g_ref, w_ref, rope_ref, q_ref, k_ref, v_ref):
    hn = _rms(x_ref[...], g_ref[...]).astype(BF16)
    cos, s_lo, s_hi = rope_ref[0], rope_ref[1], rope_ref[2]
    nq = Q_WIDTH // LANES
    for blk in range((Q_WIDTH + KV_WIDTH) // LANES):
        t = jnp.dot(hn, w_ref[:, blk * LANES:(blk + 1) * LANES], preferred_element_type=F32)
        r = (t * cos + pltpu.roll(t, ROT_HALF, axis=1) * s_lo
             + pltpu.roll(t, LANES - ROT_HALF, axis=1) * s_hi)
        if blk < nq:
            q_ref[:, blk * LANES:(blk + 1) * LANES] = (r * Q_SCALE).astype(BF16)
        else:
            k_ref[:, (blk - nq) * LANES:(blk - nq + 1) * LANES] = r.astype(BF16)
    v_ref[...] = jnp.dot(hn, w_ref[:, Q_WIDTH + KV_WIDTH:],
                         preferred_element_type=F32).astype(BF16)


def _qkv(x, g, w, seq):
    t = x.shape[0]
    per_seq = seq // QK_TM
    row = lambda i: (i, 0)
    return pl.pallas_call(
        _qkv_kernel,
        grid=(t // QK_TM,),
        in_specs=[pl.BlockSpec((QK_TM, D_MODEL), row), _resident((1, D_MODEL)),
                  _resident(w.shape),
                  pl.BlockSpec((3, QK_TM, LANES), lambda i: (0, i % per_seq, 0))],
        out_specs=[pl.BlockSpec((QK_TM, Q_WIDTH), row), pl.BlockSpec((QK_TM, KV_WIDTH), row),
                   pl.BlockSpec((QK_TM, KV_WIDTH), row)],
        out_shape=[jax.ShapeDtypeStruct((t, Q_WIDTH), BF16),
                   jax.ShapeDtypeStruct((t, KV_WIDTH), BF16),
                   jax.ShapeDtypeStruct((t, KV_WIDTH), BF16)],
        compiler_params=_params("parallel"),
        name="qkv_rope",
    )(x, g, w, jnp.asarray(_rope_tables(seq)))


AT_TQ = 512
AT_BLK = WINDOW
AT_KEYS = 3 * AT_BLK
AT_ROWS = AT_TQ + 2 * AT_BLK
GROUP = N_HEADS // N_KV_HEADS


def _attn_kernel(sink_ref, q_ref, k_ref, kp_ref, kn_ref, v_ref, vp_ref, vn_ref, o_ref,
                 klo, khi, vlo, vhi):
    i = pl.program_id(1)
    last = pl.num_programs(1) - 1
    nb = AT_TQ // AT_BLK

    for (prev, main, nxt), lo_ref, hi_ref in (((kp_ref, k_ref, kn_ref), klo, khi),
                                              ((vp_ref, v_ref, vn_ref), vlo, vhi)):
        for row0, ref in ((0, prev), (AT_BLK, main), (AT_BLK + AT_TQ, nxt)):
            nrows = ref.shape[0]
            low = lax.broadcasted_iota(jnp.int32, (nrows, LANES), 1) < HEAD_DIM
            for kvh in range(N_KV_HEADS):
                t = ref[:, (kvh // 2) * LANES:(kvh // 2 + 1) * LANES]
                zero = jnp.zeros_like(t)
                if kvh % 2 == 0:
                    lo = jnp.where(low, t, zero)
                    hi = jnp.concatenate([zero[:, :HEAD_DIM], t[:, :HEAD_DIM]], axis=1)
                else:
                    hi = jnp.where(low, zero, t)
                    lo = jnp.concatenate([t[:, HEAD_DIM:], zero[:, :HEAD_DIM]], axis=1)
                lo_ref[kvh, row0:row0 + nrows, :] = lo
                hi_ref[kvh, row0:row0 + nrows, :] = hi

    rows2 = 2 * AT_BLK
    qrow = lax.broadcasted_iota(jnp.int32, (rows2, AT_BLK), 0) & (AT_BLK - 1)
    kcol = lax.broadcasted_iota(jnp.int32, (rows2, AT_BLK), 1)
    upper = lax.broadcasted_iota(jnp.int32, (rows2, 1), 0) < AT_BLK
    low_lane = lax.broadcasted_iota(jnp.int32, (rows2, LANES), 1) < HEAD_DIM

    def body(jb, carry):
        r0 = pl.multiple_of(jb * AT_BLK, AT_BLK)
        off_prev = jnp.where((i == 0) & (jb == 0), AT_BLK, 0)
        off_next = jnp.where((i == last) & (jb == nb - 1), AT_BLK, 0)
        m_prev = kcol >= qrow + off_prev
        m_next = kcol <= qrow - off_next
        for kvh in range(N_KV_HEADS):
            col = kvh * GROUP * HEAD_DIM
            qs = jnp.concatenate([q_ref[pl.ds(r0, AT_BLK), col:col + LANES],
                                  q_ref[pl.ds(r0, AT_BLK), col + LANES:col + 2 * LANES]], axis=0)
            kc = jnp.concatenate([klo[kvh, pl.ds(r0, AT_KEYS), :],
                                  khi[kvh, pl.ds(r0, AT_KEYS), :]], axis=0)
            vc = jnp.concatenate([vlo[kvh, pl.ds(r0, AT_KEYS), :],
                                  vhi[kvh, pl.ds(r0, AT_KEYS), :]], axis=0)
            s = lax.dot_general(qs, kc, (((1,), (1,)), ((), ())),
                                preferred_element_type=F32)
            probs, rinv = [], []
            for half in range(2):
                head = kvh * GROUP + half
                sink = jnp.where(upper, sink_ref[head] * LOG2E, sink_ref[head + 2] * LOG2E)
                c0 = half * AT_KEYS
                a = jnp.where(m_prev, s[:, c0:c0 + AT_BLK], -jnp.inf)
                b = s[:, c0 + AT_BLK:c0 + 2 * AT_BLK]
                c = jnp.where(m_next, s[:, c0 + 2 * AT_BLK:c0 + 3 * AT_BLK], -jnp.inf)
                m = jnp.max(jnp.maximum(jnp.maximum(a, b), c), axis=-1, keepdims=True)
                m = jnp.maximum(m, sink)
                pa, pb, pc = jnp.exp2(a - m), jnp.exp2(b - m), jnp.exp2(c - m)
                denom = jnp.sum(pa + pb + pc, axis=-1, keepdims=True) + jnp.exp2(sink - m)
                probs += [pa, pb, pc]
                rinv.append(1.0 / denom)
            p = jnp.concatenate(probs, axis=1).astype(BF16)
            o = jnp.dot(p, vc, preferred_element_type=F32)
            o = (o * jnp.where(low_lane, rinv[0], rinv[1])).astype(BF16)
            o_ref[pl.ds(r0, AT_BLK), col:col + LANES] = o[:AT_BLK]
            o_ref[pl.ds(r0, AT_BLK), col + LANES:col + 2 * LANES] = o[AT_BLK:]
        return carry

    lax.fori_loop(0, nb, body, 0)


def _attention(q, k, v, sink, bsz, seq):
    nb = AT_TQ // AT_BLK
    n_blk = seq // AT_BLK
    main = lambda w: pl.BlockSpec((None, AT_TQ, w), lambda b, i: (b, i, 0))
    prev = pl.BlockSpec((None, AT_BLK, KV_WIDTH), lambda b, i: (b, jnp.maximum(i * nb - 1, 0), 0))
    nxt = pl.BlockSpec((None, AT_BLK, KV_WIDTH),
                       lambda b, i: (b, jnp.minimum((i + 1) * nb, n_blk - 1), 0))
    q3 = q.reshape(bsz, seq, Q_WIDTH)
    k3 = k.reshape(bsz, seq, KV_WIDTH)
    v3 = v.reshape(bsz, seq, KV_WIDTH)
    head_copy = pltpu.VMEM((N_KV_HEADS, AT_ROWS, LANES), BF16)
    out = pl.pallas_call(
        _attn_kernel,
        grid=(bsz, seq // AT_TQ),
        in_specs=[pl.BlockSpec(memory_space=pltpu.SMEM),
                  main(Q_WIDTH), main(KV_WIDTH), prev, nxt, main(KV_WIDTH), prev, nxt],
        out_specs=main(Q_WIDTH),
        out_shape=jax.ShapeDtypeStruct((bsz, seq, Q_WIDTH), BF16),
        scratch_shapes=[head_copy, head_copy, head_copy, head_copy],
        compiler_params=_params("parallel", "parallel"),
        name="window_attention",
    )(sink, q3, k3, k3, k3, v3, v3, v3)
    return out.reshape(bsz * seq, Q_WIDTH)


def _trunk(x, norm_mix, norm_mlp, norm_final, ab_w_in, ab_w_out, cv_dw_w, cv_dw_b, cv_ln_g,
           cv_ln_b, hy_short_w, hy_short_b, hy_w1, hy_b1, hy_w2, hy_b2, hy_w3, hy_b3, hy_w4,
           hy_freq, hy_decay, hy_skip, at_w_qkv, at_sink, at_w_o, mlp_w_up, mlp_w_down):
    bsz, seq, _ = x.shape
    t = bsz * seq
    x2 = x.reshape(t, D_MODEL)

    uc, uh = _in_proj(x2, norm_mix[0][None], ab_w_in[0].astype(BF16))
    y_a = _conformer_conv(uc.reshape(bsz, seq, 2 * CONV_WIDTH), cv_dw_w[0], cv_dw_b[0][None],
                          cv_ln_g[0][None], cv_ln_b[0][None])
    y_b = _hyena(uh.reshape(bsz, seq, HYENA_IN), hy_short_w[0], hy_short_b[0], hy_w1[0], hy_b1[0],
                 hy_w2[0], hy_b2[0], hy_w3[0], hy_b3[0], hy_w4[0], hy_freq[0], hy_decay[0],
                 hy_skip[0])
    x2 = _proj_mlp(x2, [y_a.reshape(t, CONV_WIDTH), y_b.reshape(t, HYENA_WIDTH)],
                   ab_w_out[0].astype(BF16), norm_mlp[0][None], mlp_w_up[0].astype(BF16),
                   mlp_w_down[0].astype(BF16))

    q, k, v = _qkv(x2, norm_mix[1][None], at_w_qkv[0].astype(BF16), seq)
    o = _attention(q, k, v, at_sink[0], bsz, seq)
    x2 = _proj_mlp(x2, [o], at_w_o[0].astype(BF16), norm_mlp[1][None],
                   mlp_w_up[1].astype(BF16), mlp_w_down[1].astype(BF16), gf=norm_final[None])
    return x2.reshape(bsz, seq, D_MODEL)


def kernel(x_prompt, x_sample, norm_mix, norm_mlp, norm_final, ab_w_in, ab_w_out, cv_dw_w, cv_dw_b,
           cv_ln_g, cv_ln_b, hy_short_w, hy_short_b, hy_w1, hy_b1, hy_w2, hy_b2, hy_w3, hy_b3, hy_w4,
           hy_freq, hy_decay, hy_skip, at_w_qkv, at_sink, at_w_o, mlp_w_up, mlp_w_down):
    weights = (norm_mix, norm_mlp, norm_final, ab_w_in, ab_w_out, cv_dw_w, cv_dw_b, cv_ln_g,
               cv_ln_b, hy_short_w, hy_short_b, hy_w1, hy_b1, hy_w2, hy_b2, hy_w3, hy_b3, hy_w4,
               hy_freq, hy_decay, hy_skip, at_w_qkv, at_sink, at_w_o, mlp_w_up, mlp_w_down)
    return (_trunk(x_prompt, *weights), _trunk(x_sample, *weights))
```

```python
import functools
import math

import numpy as np
import jax
import jax.numpy as jnp
from jax import lax
from jax.experimental import pallas as pl
from jax.experimental.pallas import tpu as pltpu

F32 = jnp.float32
BF16 = jnp.bfloat16

D_MODEL = 1024
CONV_WIDTH = 512
CONV_KERNEL = 31
HYENA_WIDTH = 512
HYENA_IN = 3 * HYENA_WIDTH
HYENA_EMB_DIM = 33
FILTER_HIDDEN = 64
N_HEADS = 16
N_KV_HEADS = 4
HEAD_DIM = 64
ROT_DIM = 16
ROPE_THETA = 500000.0
WINDOW = 128
D_FF = 4 * D_MODEL
NORM_EPS = 1e-5
LN_EPS = 1e-5
FILTER_EPS = 1e-6

LANES = 128
DFT_P = 128
DFT_PH = DFT_P // 2
VMEM_LIMIT = 56 * 1024 * 1024


def _params(*sem):
    return pltpu.CompilerParams(dimension_semantics=sem, vmem_limit_bytes=VMEM_LIMIT)


def _resident(shape):
    nd = len(shape)
    return pl.BlockSpec(shape, lambda *_: (0,) * nd, pipeline_mode=pl.Buffered(1))


def _rms(x, g):
    return x * lax.rsqrt(jnp.mean(x * x, axis=-1, keepdims=True) + NORM_EPS) * g


CC_TR = 512
CC_HALO = 16
CC_EXT = CC_TR + 2 * CC_HALO
CC_RC = 32
SUBLANES = 8
CC_SHIFT0 = CC_HALO - CONV_KERNEL // 2
CC_MAX_ALIGNED = (CC_SHIFT0 + CONV_KERNEL - 1) // SUBLANES * SUBLANES
CC_SH_ROWS = CC_TR + CC_MAX_ALIGNED


def _front_kernel(main_ref, prev_ref, next_ref, gn_ref, win_ref, w_ref, b_ref, g_ref, beta_ref,
                  sw_ref, sb_ref, o_ref, x1_ref, x2_ref, v_ref, hn_buf, hbuf, hsh, ubuf):
    i = pl.program_id(1)
    last = pl.num_programs(1) - 1
    gn = gn_ref[...]
    hn_buf[0:CC_HALO, :] = _rms(prev_ref[...], gn).astype(BF16)
    hn_buf[CC_HALO:CC_HALO + CC_TR, :] = _rms(main_ref[...], gn).astype(BF16)
    hn_buf[CC_HALO + CC_TR:, :] = _rms(next_ref[...], gn).astype(BF16)
    first_row = jnp.where(i > 0, 0, CC_HALO)
    end_row = jnp.where(i < last, CC_EXT, CC_HALO + CC_TR)
    row = lax.broadcasted_iota(jnp.int32, (CC_EXT, 1), 0)
    valid = (row >= first_row) & (row < end_row)

    uc = jnp.dot(hn_buf[...], win_ref[:, :2 * CONV_WIDTH], preferred_element_type=F32)
    hbuf[...] = jnp.where(valid, uc[:, :CONV_WIDTH] * jax.nn.sigmoid(uc[:, CONV_WIDTH:]), 0.0)
    uh = jnp.dot(hn_buf[...], win_ref[:, 2 * CONV_WIDTH:], preferred_element_type=F32)
    ubuf[...] = jnp.where(valid, uh, 0.0)

    for n, s_ref in enumerate((x1_ref, x2_ref, v_ref)):
        cs = slice(n * HYENA_WIDTH, (n + 1) * HYENA_WIDTH)
        s_ref[...] = (ubuf[CC_HALO - 1:CC_HALO - 1 + CC_TR, cs] * sw_ref[0:1, cs]
                      + ubuf[CC_HALO:CC_HALO + CC_TR, cs] * sw_ref[1:2, cs]
                      + ubuf[CC_HALO + 1:CC_HALO + 1 + CC_TR, cs] * sw_ref[2:3, cs]
                      + sb_ref[:, cs])

    for ph in range(SUBLANES):
        hsh[ph] = hbuf[ph:ph + CC_SH_ROWS, :]
    for r in range(CC_TR // CC_RC):
        acc = jnp.broadcast_to(b_ref[...], (CC_RC, CONV_WIDTH))
        for j in range(CONV_KERNEL):
            ph, lo = (j + CC_SHIFT0) % SUBLANES, r * CC_RC + (j + CC_SHIFT0) // SUBLANES * SUBLANES
            acc = acc + hsh[ph, lo:lo + CC_RC, :] * w_ref[j:j + 1, :]
        mu = jnp.mean(acc, axis=-1, keepdims=True)
        d = acc - mu
        var = jnp.mean(d * d, axis=-1, keepdims=True)
        y = d * lax.rsqrt(var + LN_EPS) * g_ref[...] + beta_ref[...]
        o_ref[r * CC_RC:(r + 1) * CC_RC, :] = (y * jax.nn.sigmoid(y)).astype(o_ref.dtype)


def _front(x, gn, w_in, cv_w, cv_b, ln_g, ln_b, short_w, short_b):
    bsz, seq, _ = x.shape
    nh = CC_TR // CC_HALO
    n_halo = seq // CC_HALO
    tile = lambda w: pl.BlockSpec((None, CC_TR, w), lambda bi, i: (bi, i, 0))
    f32_out = jax.ShapeDtypeStruct((bsz, seq, HYENA_WIDTH), F32)
    return pl.pallas_call(
        _front_kernel,
        grid=(bsz, seq // CC_TR),
        in_specs=[
            tile(D_MODEL),
            pl.BlockSpec((None, CC_HALO, D_MODEL),
                         lambda bi, i: (bi, jnp.maximum(i * nh - 1, 0), 0)),
            pl.BlockSpec((None, CC_HALO, D_MODEL),
                         lambda bi, i: (bi, jnp.minimum((i + 1) * nh, n_halo - 1), 0)),
            _resident((1, D_MODEL)), _resident(w_in.shape),
            _resident((CONV_KERNEL, CONV_WIDTH)), _resident((1, CONV_WIDTH)),
            _resident((1, CONV_WIDTH)), _resident((1, CONV_WIDTH)),
            _resident((3, HYENA_IN)), _resident((1, HYENA_IN)),
        ],
        out_specs=[tile(CONV_WIDTH), tile(HYENA_WIDTH), tile(HYENA_WIDTH), tile(HYENA_WIDTH)],
        out_shape=[jax.ShapeDtypeStruct((bsz, seq, CONV_WIDTH), BF16), f32_out, f32_out, f32_out],
        scratch_shapes=[pltpu.VMEM((CC_EXT, D_MODEL), BF16),
                        pltpu.VMEM((CC_EXT, CONV_WIDTH), F32),
                        pltpu.VMEM((SUBLANES, CC_SH_ROWS, CONV_WIDTH), F32),
                        pltpu.VMEM((CC_EXT, HYENA_IN), F32)],
        compiler_params=_params("parallel", "parallel"),
        name="front",
    )(x, x, x, gn, w_in, cv_w, cv_b, ln_g, ln_b, short_w, short_b)


FM_TL = 512
FEAT = 64
MASK_COL = HYENA_EMB_DIM
N_FILT = 2 * HYENA_WIDTH


@functools.lru_cache(maxsize=None)
def _filter_features(seq):
    t = np.linspace(0.0, 1.0, seq, dtype=np.float32).astype(np.float64)
    bands = (HYENA_EMB_DIM - 1) // 2
    w = (2.0 * math.pi * np.arange(seq, dtype=np.float32) / np.float32(seq)).astype(np.float64)
    f = np.linspace(1e-4, bands - 1, bands, dtype=np.float32).astype(np.float64)
    fw = w[:, None] * f[None, :]
    z = np.concatenate([t[:, None], np.cos(fw), -np.sin(fw)], axis=-1)
    fwd = np.zeros((seq, FEAT), np.float64)
    fwd[:, :HYENA_EMB_DIM] = z
    fwd[:, MASK_COL] = 1.0
    bwd = np.zeros((seq, FEAT), np.float64)
    bwd[1:, :HYENA_EMB_DIM] = z[:0:-1]
    bwd[1:, MASK_COL] = 1.0
    return np.concatenate([fwd, bwd], axis=-1).astype(np.float32)


def _filter_mlp_kernel(z_ref, w1_ref, b1_ref, w2_ref, b2_ref, w3_ref, b3_ref, w4_ref, fr_ref,
                       dec_ref, k_ref, ssq_ref):
    hp = lax.Precision.HIGHEST
    z = z_ref[...]
    h = jnp.sin(fr_ref[0:1, :] * (jnp.dot(z, w1_ref[...], precision=hp,
                                          preferred_element_type=F32) + b1_ref[...]))
    h = jnp.sin(fr_ref[1:2, :] * (jnp.dot(h, w2_ref[...], precision=hp,
                                          preferred_element_type=F32) + b2_ref[...]))
    h = jnp.sin(fr_ref[2:3, :] * (jnp.dot(h, w3_ref[...], precision=hp,
                                          preferred_element_type=F32) + b3_ref[...]))
    ssq = jnp.zeros((1, N_FILT), F32)
    hb = h.astype(BF16)
    for d in range(2):
        k = jnp.dot(hb, w4_ref[d], preferred_element_type=F32)
        t = z[:, d * FEAT:d * FEAT + 1]
        mask = z[:, d * FEAT + MASK_COL:d * FEAT + MASK_COL + 1]
        k = k * jnp.exp(-t * jnp.abs(dec_ref[d])) * mask
        k_ref[d] = k
        ssq = ssq + jnp.sum(k * k, axis=0, keepdims=True)

    @pl.when(pl.program_id(0) == 0)
    def _():
        ssq_ref[...] = jnp.zeros_like(ssq_ref)

    ssq_ref[...] += ssq


def _filter_mlp(seq, w1, b1, w2, b2, w3, b3, w4, freq, decay):
    fh = FILTER_HIDDEN

    def blockdiag(w):
        z = jnp.zeros_like(w)
        return jnp.concatenate([jnp.concatenate([w, z], 1), jnp.concatenate([z, w], 1)], 0)

    w1p = jnp.zeros((FEAT, fh), F32).at[:HYENA_EMB_DIM].set(w1)
    w4r = w4.reshape(fh, 2, 2, HYENA_WIDTH)
    zeros = jnp.zeros((fh, N_FILT), F32)
    w4f = jnp.concatenate([w4r[:, :, 0].reshape(fh, N_FILT), zeros], 0)
    w4b = jnp.concatenate([zeros, w4r[:, :, 1].reshape(fh, N_FILT)], 0)
    dec = decay.reshape(2, 2, HYENA_WIDTH)
    args = (
        jnp.asarray(_filter_features(seq)),
        blockdiag(w1p), jnp.tile(b1, 2)[None], blockdiag(w2), jnp.tile(b2, 2)[None],
        blockdiag(w3), jnp.tile(b3, 2)[None],
        jnp.stack([w4f, w4b]).astype(BF16),
        jnp.tile(freq, (1, 2)),
        jnp.stack([dec[:, 0].reshape(1, N_FILT), dec[:, 1].reshape(1, N_FILT)]),
    )
    in_specs = [pl.BlockSpec((FM_TL, 2 * FEAT), lambda i: (i, 0))]
    in_specs += [_resident(a.shape) for a in args[1:]]
    return pl.pallas_call(
        _filter_mlp_kernel,
        grid=(seq // FM_TL,),
        in_specs=in_specs,
        out_specs=[pl.BlockSpec((2, FM_TL, N_FILT), lambda i: (0, i, 0)),
                   pl.BlockSpec((1, N_FILT), lambda i: (0, 0))],
        out_shape=[jax.ShapeDtypeStruct((2, seq, N_FILT), F32),
                   jax.ShapeDtypeStruct((1, N_FILT), F32)],
        compiler_params=_params("arbitrary"),
        name="filter_mlp",
    )(*args)


@functools.lru_cache(maxsize=None)
def _dft_tables(q):
    n = DFT_P * q
    c = np.arange(DFT_P)[:, None]
    a = np.arange(DFT_P)[None, :]
    b = np.arange(q)[:, None, None]
    ang = -2.0 * np.pi * ((c * (q * a + b)) % n) / n
    fr, fi = np.cos(ang), np.sin(ang)
    d = np.arange(q)[:, None]
    ang2 = -2.0 * np.pi * ((d * np.arange(q)[None, :]) % q) / q
    gr, gi = np.cos(ang2), np.sin(ang2)
    g2 = np.block([[gr, -gi], [gi, gr]])
    g2i = np.block([[gr, gi], [-gi, gr]])
    return (fr.astype(np.float32), fi.astype(np.float32),
            g2.astype(np.float32), g2i.astype(np.float32))


def _dft_mats(q):
    fr, fi, g2, g2i = (jnp.asarray(t) for t in _dft_tables(q))
    frh, fih = fr[:, :, :DFT_PH], fi[:, :, :DFT_PH]
    m1 = jnp.concatenate([jnp.concatenate([frh, -fih], 2), jnp.concatenate([fih, frh], 2)], 1)
    m1f = jnp.concatenate([fr, fi], 1)
    hr, hi = jnp.swapaxes(frh, 1, 2), -jnp.swapaxes(fih, 1, 2)
    m3 = jnp.concatenate([jnp.concatenate([hr, -hi], 2), jnp.concatenate([hi, hr], 2)], 1)
    return (m1.astype(BF16), m1f.astype(BF16), m3.astype(BF16),
            g2.astype(BF16), g2i.astype(BF16))


S1_NB = 16
S1_CW = 256


def _dft_s1_kernel(x_ref, m_ref, ar_ref, ai_ref, re_buf, im_buf):
    x = x_ref[...].astype(BF16).reshape(DFT_P, S1_NB, S1_CW)
    xs = jnp.swapaxes(x, 0, 1)
    for j in range(S1_NB):
        r = jnp.dot(m_ref[j], xs[j], preferred_element_type=F32)
        re_buf[j] = r[:DFT_P].astype(BF16)
        im_buf[j] = r[DFT_P:].astype(BF16)
    ar_ref[...] = jnp.swapaxes(re_buf[...], 0, 1)
    ai_ref[...] = jnp.swapaxes(im_buf[...], 0, 1)


def _dft_stage1(x, m, q):
    g, wtot = x.shape[0] // 2, x.shape[3]
    out = jax.ShapeDtypeStruct((g, DFT_P, q, wtot), BF16)
    ospec = pl.BlockSpec((None, DFT_P, S1_NB, S1_CW), lambda gi, j, ci: (gi, 0, j, ci))
    stage = pltpu.VMEM((S1_NB, DFT_P, S1_CW), BF16)
    return pl.pallas_call(
        _dft_s1_kernel,
        grid=(g, q // S1_NB, wtot // S1_CW),
        in_specs=[pl.BlockSpec((2, DFT_PH, S1_NB, S1_CW), lambda gi, j, ci: (gi, 0, j, ci)),
                  pl.BlockSpec((S1_NB, 2 * DFT_P, DFT_P), lambda gi, j, ci: (j, 0, 0))],
        out_specs=[ospec, ospec],
        out_shape=[out, out],
        scratch_shapes=[stage, stage],
        compiler_params=_params("parallel", "parallel", "parallel"),
        name="dft_stage1",
    )(x, m)


S2_NC = 8


def _filter_s2_kernel(ar_ref, ai_ref, g2_ref, ssq_ref, kr_ref, ki_ref, *, q, inv_n):
    scale = lax.rsqrt(ssq_ref[...] + FILTER_EPS) * inv_n
    for c in range(S2_NC):
        a = jnp.concatenate([ar_ref[c], ai_ref[c]], axis=0)
        x = jnp.dot(g2_ref[...], a, preferred_element_type=F32)
        kr_ref[c] = x[:q] * scale
        ki_ref[c] = x[q:] * scale


def _filter_stage2(ar, ai, g2, ssq, q):
    spec = pl.BlockSpec((S2_NC, q, N_FILT), lambda i: (i, 0, 0))
    out = jax.ShapeDtypeStruct((DFT_P, q, N_FILT), F32)
    return pl.pallas_call(
        functools.partial(_filter_s2_kernel, q=q, inv_n=1.0 / (DFT_P * q)),
        grid=(DFT_P // S2_NC,),
        in_specs=[spec, spec, _resident((2 * q, 2 * q)), _resident((1, N_FILT))],
        out_specs=[spec, spec],
        out_shape=[out, out],
        compiler_params=_params("parallel"),
        name="filter_stage2",
    )(ar, ai, g2, ssq)


def _dft_mid_kernel(ar_ref, ai_ref, g2_ref, g2i_ref, kr_ref, ki_ref, yr_ref, yi_ref, *, q):
    for c in range(S2_NC):
        a = jnp.concatenate([ar_ref[c], ai_ref[c]], axis=0)
        x = jnp.dot(g2_ref[...], a, preferred_element_type=F32)
        xr, xi = x[:q], x[q:]
        kr, ki = kr_ref[c], ki_ref[c]
        z = jnp.concatenate([xr * kr - xi * ki, xr * ki + xi * kr], axis=0).astype(BF16)
        y = jnp.dot(g2i_ref[...], z, preferred_element_type=F32)
        yr_ref[c] = y[:q].astype(BF16)
        yi_ref[c] = y[q:].astype(BF16)


def _dft_mid(ar, ai, g2, g2i, kr, ki, order, q):
    pairs = ar.shape[0]
    cw = HYENA_WIDTH
    spec = pl.BlockSpec((None, S2_NC, q, cw), lambda i, p: (p, i, 0, 0))
    kspec = pl.BlockSpec((S2_NC, q, cw), lambda i, p: (i, 0, order))
    out = jax.ShapeDtypeStruct(ar.shape, BF16)
    return pl.pallas_call(
        functools.partial(_dft_mid_kernel, q=q),
        grid=(DFT_P // S2_NC, pairs),
        in_specs=[spec, spec, _resident((2 * q, 2 * q)), _resident((2 * q, 2 * q)), kspec, kspec],
        out_specs=[spec, spec],
        out_shape=[out, out],
        compiler_params=_params("parallel", "parallel"),
        name="dft_mid",
    )(ar, ai, g2, g2i, kr, ki)


def _dft_s3_kernel(yr_ref, yi_ref, m_ref, gate_ref, z_ref, sk_ref, o_ref, y_buf):
    yr = jnp.swapaxes(yr_ref[...], 0, 1)
    yi = jnp.swapaxes(yi_ref[...], 0, 1)
    for j in range(S1_NB):
        yc = jnp.concatenate([yr[j], yi[j]], axis=0)
        y_buf[j] = jnp.dot(m_ref[j], yc, preferred_element_type=F32)
    y = jnp.swapaxes(y_buf[...], 0, 1).reshape(2, DFT_PH, S1_NB, S1_CW)
    o_ref[...] = (gate_ref[...] * (y + z_ref[...] * sk_ref[...])).astype(o_ref.dtype)


def _dft_stage3(yr, yi, m3, gate, z, sk, q, out_dtype):
    pairs, cw = yr.shape[0], yr.shape[3]
    yspec = pl.BlockSpec((None, DFT_P, S1_NB, S1_CW), lambda p, j, ci: (p, 0, j, ci))
    tspec = pl.BlockSpec((2, DFT_PH, S1_NB, S1_CW), lambda p, j, ci: (p, 0, j, ci))
    return pl.pallas_call(
        _dft_s3_kernel,
        grid=(pairs, q // S1_NB, cw // S1_CW),
        in_specs=[yspec, yspec,
                  pl.BlockSpec((S1_NB, DFT_P, 2 * DFT_P), lambda p, j, ci: (j, 0, 0)),
                  tspec, tspec, pl.BlockSpec((1, S1_CW), lambda p, j, ci: (0, ci))],
        out_specs=tspec,
        out_shape=jax.ShapeDtypeStruct(z.shape, out_dtype),
        scratch_shapes=[pltpu.VMEM((S1_NB, DFT_P, S1_CW), F32)],
        compiler_params=_params("parallel", "parallel", "parallel"),
        name="dft_stage3",
    )(yr, yi, m3, gate, z, sk)


def _hyena(x1, x2, v, w1, b1, w2, b2, w3, b3, w4, freq, decay, skip):
    bsz, seq, _ = v.shape
    q = 2 * seq // DFT_P
    cw = HYENA_WIDTH
    m1, m1f, m3, g2, g2i = _dft_mats(q)

    k_time, ssq = _filter_mlp(seq, w1, b1, w2, b2, w3, b3, w4, freq, decay)
    afr, afi = _dft_stage1(k_time.reshape(2, DFT_PH, q, N_FILT), m1f, q)
    kr, ki = _filter_stage2(afr.reshape(DFT_P, q, N_FILT), afi.reshape(DFT_P, q, N_FILT),
                            g2, ssq, q)

    time_view = (bsz, DFT_PH, q, cw)
    z = v.reshape(time_view)
    gates = (x1.reshape(time_view), x2.reshape(time_view))
    for n in range(2):
        ar, ai = _dft_stage1(z, m1, q)
        yr, yi = _dft_mid(ar, ai, g2, g2i, kr, ki, n, q)
        z = _dft_stage3(yr, yi, m3, gates[n], z, skip[n][None], q, F32 if n == 0 else BF16)
    return z.reshape(bsz, seq, cw)


PM_TM = 512
PM_FC = 1024


def _proj_mlp_kernel(*refs, n_parts, final):
    x_ref = refs[0]
    part_refs = refs[1:1 + n_parts]
    wo_ref, gm_ref, wup_ref, wdn_ref = refs[1 + n_parts:5 + n_parts]
    gf_ref = refs[5 + n_parts] if final else None
    o_ref = refs[-1]
    mixed = jnp.concatenate([p_ref[...] for p_ref in part_refs], axis=1)
    x = x_ref[...] + jnp.dot(mixed, wo_ref[...], preferred_element_type=F32)
    hn = _rms(x, gm_ref[...]).astype(BF16)
    for ch in range(D_FF // PM_FC):
        cs = slice(ch * PM_FC, (ch + 1) * PM_FC)
        h = jnp.dot(hn, wup_ref[:, cs], preferred_element_type=F32)
        h = jnp.square(jnp.maximum(h, 0.0)).astype(BF16)
        x = x + jnp.dot(h, wdn_ref[cs, :], preferred_element_type=F32)
    if final:
        x = _rms(x, gf_ref[...])
    o_ref[...] = x


def _proj_mlp(x, parts, wo, gm, wup, wdn, gf=None):
    t = x.shape[0]
    final = gf is not None
    row = lambda i: (i, 0)
    in_specs = [pl.BlockSpec((PM_TM, D_MODEL), row)]
    in_specs += [pl.BlockSpec((PM_TM, p.shape[1]), row) for p in parts]
    in_specs += [_resident(wo.shape), _resident((1, D_MODEL)), _resident(wup.shape),
                 _resident(wdn.shape)]
    args = [x, *parts, wo, gm, wup, wdn]
    if final:
        in_specs.append(_resident((1, D_MODEL)))
        args.append(gf)
    return pl.pallas_call(
        functools.partial(_proj_mlp_kernel, n_parts=len(parts), final=final),
        grid=(t // PM_TM,),
        in_specs=in_specs,
        out_specs=pl.BlockSpec((PM_TM, D_MODEL), row),
        out_shape=jax.ShapeDtypeStruct((t, D_MODEL), F32),
        compiler_params=_params("parallel"),
        name="proj_mlp",
    )(*args)


QK_TM = 512
Q_WIDTH = N_HEADS * HEAD_DIM
KV_WIDTH = N_KV_HEADS * HEAD_DIM
ROT_HALF = ROT_DIM // 2
LOG2E = math.log2(math.e)
Q_SCALE = HEAD_DIM ** -0.5 * LOG2E


@functools.lru_cache(maxsize=None)
def _rope_tables(seq):
    inv = ROPE_THETA ** (-(np.arange(0, ROT_DIM, 2, dtype=np.float64) / ROT_DIM))
    ang = np.arange(seq, dtype=np.float64)[:, None] * inv[None, :]
    cos = np.ones((seq, HEAD_DIM))
    s_lo = np.zeros((seq, HEAD_DIM))
    s_hi = np.zeros((seq, HEAD_DIM))
    cos[:, :ROT_HALF] = np.cos(ang)
    cos[:, ROT_HALF:ROT_DIM] = np.cos(ang)
    s_lo[:, ROT_HALF:ROT_DIM] = np.sin(ang)
    s_hi[:, :ROT_HALF] = -np.sin(ang)
    rep = LANES // HEAD_DIM
    return np.stack([np.tile(cos, (1, rep)), np.tile(s_lo, (1, rep)),
                     np.tile(s_hi, (1, rep))]).astype(np.float32)


def _qkv_kernel(x_ref, g_ref, w_ref, rope_ref, q_ref, k_ref, v_ref):
    hn = _rms(x_ref[...], g_ref[...]).astype(BF16)
    cos, s_lo, s_hi = rope_ref[0], rope_ref[1], rope_ref[2]
    nq = Q_WIDTH // LANES
    qk = jnp.dot(hn, w_ref[:, :Q_WIDTH + KV_WIDTH], preferred_element_type=F32)
    for blk in range((Q_WIDTH + KV_WIDTH) // LANES):
        t = qk[:, blk * LANES:(blk + 1) * LANES]
        r = (t * cos + pltpu.roll(t, ROT_HALF, axis=1) * s_lo
             + pltpu.roll(t, LANES - ROT_HALF, axis=1) * s_hi)
        if blk < nq:
            q_ref[:, blk * LANES:(blk + 1) * LANES] = (r * Q_SCALE).astype(BF16)
        else:
            k_ref[:, (blk - nq) * LANES:(blk - nq + 1) * LANES] = r.astype(BF16)
    v_ref[...] = jnp.dot(hn, w_ref[:, Q_WIDTH + KV_WIDTH:],
                         preferred_element_type=F32).astype(BF16)


def _qkv(x, g, w, seq):
    t = x.shape[0]
    per_seq = seq // QK_TM
    row = lambda i: (i, 0)
    return pl.pallas_call(
        _qkv_kernel,
        grid=(t // QK_TM,),
        in_specs=[pl.BlockSpec((QK_TM, D_MODEL), row), _resident((1, D_MODEL)),
                  _resident(w.shape),
                  pl.BlockSpec((3, QK_TM, LANES), lambda i: (0, i % per_seq, 0))],
        out_specs=[pl.BlockSpec((QK_TM, Q_WIDTH), row), pl.BlockSpec((QK_TM, KV_WIDTH), row),
                   pl.BlockSpec((QK_TM, KV_WIDTH), row)],
        out_shape=[jax.ShapeDtypeStruct((t, Q_WIDTH), BF16),
                   jax.ShapeDtypeStruct((t, KV_WIDTH), BF16),
                   jax.ShapeDtypeStruct((t, KV_WIDTH), BF16)],
        compiler_params=_params("parallel"),
        name="qkv_rope",
    )(x, g, w, jnp.asarray(_rope_tables(seq)))


AT_TQ = 512
AT_BLK = WINDOW
AT_KEYS = 3 * AT_BLK
AT_ROWS = AT_TQ + 2 * AT_BLK
GROUP = N_HEADS // N_KV_HEADS


def _attn_kernel(sink_ref, q_ref, k_ref, kp_ref, kn_ref, v_ref, vp_ref, vn_ref, o_ref,
                 klo, khi, vlo, vhi):
    i = pl.program_id(1)
    last = pl.num_programs(1) - 1
    nb = AT_TQ // AT_BLK

    for (prev, main, nxt), lo_ref, hi_ref in (((kp_ref, k_ref, kn_ref), klo, khi),
                                              ((vp_ref, v_ref, vn_ref), vlo, vhi)):
        for row0, ref in ((0, prev), (AT_BLK, main), (AT_BLK + AT_TQ, nxt)):
            nrows = ref.shape[0]
            low = lax.broadcasted_iota(jnp.int32, (nrows, LANES), 1) < HEAD_DIM
            for kvh in range(N_KV_HEADS):
                t = ref[:, (kvh // 2) * LANES:(kvh // 2 + 1) * LANES]
                zero = jnp.zeros_like(t)
                if kvh % 2 == 0:
                    lo = jnp.where(low, t, zero)
                    hi = jnp.concatenate([zero[:, :HEAD_DIM], t[:, :HEAD_DIM]], axis=1)
                else:
                    hi = jnp.where(low, zero, t)
                    lo = jnp.concatenate([t[:, HEAD_DIM:], zero[:, :HEAD_DIM]], axis=1)
                lo_ref[kvh, row0:row0 + nrows, :] = lo
                hi_ref[kvh, row0:row0 + nrows, :] = hi

    rows2 = 2 * AT_BLK
    qrow = lax.broadcasted_iota(jnp.int32, (rows2, AT_BLK), 0) & (AT_BLK - 1)
    kcol = lax.broadcasted_iota(jnp.int32, (rows2, AT_BLK), 1)
    upper = lax.broadcasted_iota(jnp.int32, (rows2, 1), 0) < AT_BLK
    low_lane = lax.broadcasted_iota(jnp.int32, (rows2, LANES), 1) < HEAD_DIM

    def body(jb, carry):
        r0 = pl.multiple_of(jb * AT_BLK, AT_BLK)
        off_prev = jnp.where((i == 0) & (jb == 0), AT_BLK, 0)
        off_next = jnp.where((i == last) & (jb == nb - 1), AT_BLK, 0)
        m_prev = kcol >= qrow + off_prev
        m_next = kcol <= qrow - off_next
        for kvh in range(N_KV_HEADS):
            col = kvh * GROUP * HEAD_DIM
            qs = jnp.concatenate([q_ref[pl.ds(r0, AT_BLK), col:col + LANES],
                                  q_ref[pl.ds(r0, AT_BLK), col + LANES:col + 2 * LANES]], axis=0)
            kc = jnp.concatenate([klo[kvh, pl.ds(r0, AT_KEYS), :],
                                  khi[kvh, pl.ds(r0, AT_KEYS), :]], axis=0)
            vc = jnp.concatenate([vlo[kvh, pl.ds(r0, AT_KEYS), :],
                                  vhi[kvh, pl.ds(r0, AT_KEYS), :]], axis=0)
            s = lax.dot_general(qs, kc, (((1,), (1,)), ((), ())),
                                preferred_element_type=F32)
            probs, rinv = [], []
            for half in range(2):
                head = kvh * GROUP + half
                sink = jnp.where(upper, sink_ref[head] * LOG2E, sink_ref[head + 2] * LOG2E)
                c0 = half * AT_KEYS
                a = jnp.where(m_prev, s[:, c0:c0 + AT_BLK], -jnp.inf)
                b = s[:, c0 + AT_BLK:c0 + 2 * AT_BLK]
                c = jnp.where(m_next, s[:, c0 + 2 * AT_BLK:c0 + 3 * AT_BLK], -jnp.inf)
                m = jnp.max(jnp.maximum(jnp.maximum(a, b), c), axis=-1, keepdims=True)
                m = jnp.maximum(m, sink)
                pa, pb, pc = jnp.exp2(a - m), jnp.exp2(b - m), jnp.exp2(c - m)
                denom = jnp.sum(pa + pb + pc, axis=-1, keepdims=True) + jnp.exp2(sink - m)
                probs += [pa, pb, pc]
                rinv.append(1.0 / denom)
            p = jnp.concatenate(probs, axis=1).astype(BF16)
            o = jnp.dot(p, vc, preferred_element_type=F32)
            o = (o * jnp.where(low_lane, rinv[0], rinv[1])).astype(BF16)
            o_ref[pl.ds(r0, AT_BLK), col:col + LANES] = o[:AT_BLK]
            o_ref[pl.ds(r0, AT_BLK), col + LANES:col + 2 * LANES] = o[AT_BLK:]
        return carry

    lax.fori_loop(0, nb, body, 0)


def _attention(q, k, v, sink, bsz, seq):
    nb = AT_TQ // AT_BLK
    n_blk = seq // AT_BLK
    main = lambda w: pl.BlockSpec((None, AT_TQ, w), lambda b, i: (b, i, 0))
    prev = pl.BlockSpec((None, AT_BLK, KV_WIDTH), lambda b, i: (b, jnp.maximum(i * nb - 1, 0), 0))
    nxt = pl.BlockSpec((None, AT_BLK, KV_WIDTH),
                       lambda b, i: (b, jnp.minimum((i + 1) * nb, n_blk - 1), 0))
    q3 = q.reshape(bsz, seq, Q_WIDTH)
    k3 = k.reshape(bsz, seq, KV_WIDTH)
    v3 = v.reshape(bsz, seq, KV_WIDTH)
    head_copy = pltpu.VMEM((N_KV_HEADS, AT_ROWS, LANES), BF16)
    out = pl.pallas_call(
        _attn_kernel,
        grid=(bsz, seq // AT_TQ),
        in_specs=[pl.BlockSpec(memory_space=pltpu.SMEM),
                  main(Q_WIDTH), main(KV_WIDTH), prev, nxt, main(KV_WIDTH), prev, nxt],
        out_specs=main(Q_WIDTH),
        out_shape=jax.ShapeDtypeStruct((bsz, seq, Q_WIDTH), BF16),
        scratch_shapes=[head_copy, head_copy, head_copy, head_copy],
        compiler_params=_params("parallel", "parallel"),
        name="window_attention",
    )(sink, q3, k3, k3, k3, v3, v3, v3)
    return out.reshape(bsz * seq, Q_WIDTH)


def _trunk(x, norm_mix, norm_mlp, norm_final, ab_w_in, ab_w_out, cv_dw_w, cv_dw_b, cv_ln_g,
           cv_ln_b, hy_short_w, hy_short_b, hy_w1, hy_b1, hy_w2, hy_b2, hy_w3, hy_b3, hy_w4,
           hy_freq, hy_decay, hy_skip, at_w_qkv, at_sink, at_w_o, mlp_w_up, mlp_w_down):
    bsz, seq, _ = x.shape
    t = bsz * seq
    x2 = x.reshape(t, D_MODEL)

    y_a, gate1, gate2, v = _front(x, norm_mix[0][None], ab_w_in[0].astype(BF16), cv_dw_w[0],
                                  cv_dw_b[0][None], cv_ln_g[0][None], cv_ln_b[0][None],
                                  hy_short_w[0], hy_short_b[0][None])
    y_b = _hyena(gate1, gate2, v, hy_w1[0], hy_b1[0], hy_w2[0], hy_b2[0], hy_w3[0], hy_b3[0],
                 hy_w4[0], hy_freq[0], hy_decay[0], hy_skip[0])
    x2 = _proj_mlp(x2, [y_a.reshape(t, CONV_WIDTH), y_b.reshape(t, HYENA_WIDTH)],
                   ab_w_out[0].astype(BF16), norm_mlp[0][None], mlp_w_up[0].astype(BF16),
                   mlp_w_down[0].astype(BF16))

    q, k, v = _qkv(x2, norm_mix[1][None], at_w_qkv[0].astype(BF16), seq)
    o = _attention(q, k, v, at_sink[0], bsz, seq)
    x2 = _proj_mlp(x2, [o], at_w_o[0].astype(BF16), norm_mlp[1][None],
                   mlp_w_up[1].astype(BF16), mlp_w_down[1].astype(BF16), gf=norm_final[None])
    return x2.reshape(bsz, seq, D_MODEL)


def kernel(x_prompt, x_sample, norm_mix, norm_mlp, norm_final, ab_w_in, ab_w_out, cv_dw_w, cv_dw_b,
           cv_ln_g, cv_ln_b, hy_short_w, hy_short_b, hy_w1, hy_b1, hy_w2, hy_b2, hy_w3, hy_b3, hy_w4,
           hy_freq, hy_decay, hy_skip, at_w_qkv, at_sink, at_w_o, mlp_w_up, mlp_w_down):
    weights = (norm_mix, norm_mlp, norm_final, ab_w_in, ab_w_out, cv_dw_w, cv_dw_b, cv_ln_g,
               cv_ln_b, hy_short_w, hy_short_b, hy_w1, hy_b1, hy_w2, hy_b2, hy_w3, hy_b3, hy_w4,
               hy_freq, hy_decay, hy_skip, at_w_qkv, at_sink, at_w_o, mlp_w_up, mlp_w_down)
    return (_trunk(x_prompt, *weights), _trunk(x_sample, *weights))
```

```python
import functools
import math

import numpy as np
import jax
import jax.numpy as jnp
from jax import lax
from jax.experimental import pallas as pl
from jax.experimental.pallas import tpu as pltpu

F32 = jnp.float32
BF16 = jnp.bfloat16

D_MODEL = 1024
CONV_WIDTH = 512
CONV_KERNEL = 31
HYENA_WIDTH = 512
HYENA_IN = 3 * HYENA_WIDTH
HYENA_EMB_DIM = 33
FILTER_HIDDEN = 64
N_HEADS = 16
N_KV_HEADS = 4
HEAD_DIM = 64
ROT_DIM = 16
ROPE_THETA = 500000.0
WINDOW = 128
D_FF = 4 * D_MODEL
NORM_EPS = 1e-5
LN_EPS = 1e-5
FILTER_EPS = 1e-6

LANES = 128
DFT_P = 128
DFT_PH = DFT_P // 2
VMEM_LIMIT = 56 * 1024 * 1024


def _params(*sem):
    return pltpu.CompilerParams(dimension_semantics=sem, vmem_limit_bytes=VMEM_LIMIT)


def _resident(shape):
    nd = len(shape)
    return pl.BlockSpec(shape, lambda *_: (0,) * nd, pipeline_mode=pl.Buffered(1))


def _rms(x, g):
    return x * lax.rsqrt(jnp.mean(x * x, axis=-1, keepdims=True) + NORM_EPS) * g


CC_TR = 512
CC_HALO = 16
CC_EXT = CC_TR + 2 * CC_HALO
CC_RC = 32
SUBLANES = 8
CC_SHIFT0 = CC_HALO - CONV_KERNEL // 2
CC_MAX_ALIGNED = (CC_SHIFT0 + CONV_KERNEL - 1) // SUBLANES * SUBLANES
CC_SH_ROWS = CC_TR + CC_MAX_ALIGNED


def _front_kernel(main_ref, prev_ref, next_ref, gn_ref, win_ref, w_ref, b_ref, g_ref, beta_ref,
                  sw_ref, sb_ref, o_ref, x1_ref, x2_ref, v_ref, hn_buf, hbuf, hsh, ubuf):
    i = pl.program_id(1)
    last = pl.num_programs(1) - 1
    gn = gn_ref[...]
    hn_buf[0:CC_HALO, :] = _rms(prev_ref[...], gn).astype(BF16)
    hn_buf[CC_HALO:CC_HALO + CC_TR, :] = _rms(main_ref[...], gn).astype(BF16)
    hn_buf[CC_HALO + CC_TR:, :] = _rms(next_ref[...], gn).astype(BF16)
    first_row = jnp.where(i > 0, 0, CC_HALO)
    end_row = jnp.where(i < last, CC_EXT, CC_HALO + CC_TR)
    row = lax.broadcasted_iota(jnp.int32, (CC_EXT, 1), 0)
    valid = (row >= first_row) & (row < end_row)

    uc = jnp.dot(hn_buf[...], win_ref[:, :2 * CONV_WIDTH], preferred_element_type=F32)
    hbuf[...] = jnp.where(valid, uc[:, :CONV_WIDTH] * jax.nn.sigmoid(uc[:, CONV_WIDTH:]), 0.0)
    uh = jnp.dot(hn_buf[...], win_ref[:, 2 * CONV_WIDTH:], preferred_element_type=F32)
    ubuf[...] = jnp.where(valid, uh, 0.0)

    for n, s_ref in enumerate((x1_ref, x2_ref, v_ref)):
        cs = slice(n * HYENA_WIDTH, (n + 1) * HYENA_WIDTH)
        s_ref[...] = (ubuf[CC_HALO - 1:CC_HALO - 1 + CC_TR, cs] * sw_ref[0:1, cs]
                      + ubuf[CC_HALO:CC_HALO + CC_TR, cs] * sw_ref[1:2, cs]
                      + ubuf[CC_HALO + 1:CC_HALO + 1 + CC_TR, cs] * sw_ref[2:3, cs]
                      + sb_ref[:, cs]).astype(s_ref.dtype)

    for ph in range(SUBLANES):
        hsh[ph] = hbuf[ph:ph + CC_SH_ROWS, :]
    for r in range(CC_TR // CC_RC):
        acc = jnp.broadcast_to(b_ref[...], (CC_RC, CONV_WIDTH))
        for j in range(CONV_KERNEL):
            ph, lo = (j + CC_SHIFT0) % SUBLANES, r * CC_RC + (j + CC_SHIFT0) // SUBLANES * SUBLANES
            acc = acc + hsh[ph, lo:lo + CC_RC, :] * w_ref[j:j + 1, :]
        mu = jnp.mean(acc, axis=-1, keepdims=True)
        d = acc - mu
        var = jnp.mean(d * d, axis=-1, keepdims=True)
        y = d * lax.rsqrt(var + LN_EPS) * g_ref[...] + beta_ref[...]
        o_ref[r * CC_RC:(r + 1) * CC_RC, :] = (y * jax.nn.sigmoid(y)).astype(o_ref.dtype)


def _front(x, gn, w_in, cv_w, cv_b, ln_g, ln_b, short_w, short_b):
    bsz, seq, _ = x.shape
    nh = CC_TR // CC_HALO
    n_halo = seq // CC_HALO
    tile = lambda w: pl.BlockSpec((None, CC_TR, w), lambda bi, i: (bi, i, 0))
    hy_out = jax.ShapeDtypeStruct((bsz, seq, HYENA_WIDTH), BF16)
    return pl.pallas_call(
        _front_kernel,
        grid=(bsz, seq // CC_TR),
        in_specs=[
            tile(D_MODEL),
            pl.BlockSpec((None, CC_HALO, D_MODEL),
                         lambda bi, i: (bi, jnp.maximum(i * nh - 1, 0), 0)),
            pl.BlockSpec((None, CC_HALO, D_MODEL),
                         lambda bi, i: (bi, jnp.minimum((i + 1) * nh, n_halo - 1), 0)),
            _resident((1, D_MODEL)), _resident(w_in.shape),
            _resident((CONV_KERNEL, CONV_WIDTH)), _resident((1, CONV_WIDTH)),
            _resident((1, CONV_WIDTH)), _resident((1, CONV_WIDTH)),
            _resident((3, HYENA_IN)), _resident((1, HYENA_IN)),
        ],
        out_specs=[tile(CONV_WIDTH), tile(HYENA_WIDTH), tile(HYENA_WIDTH), tile(HYENA_WIDTH)],
        out_shape=[jax.ShapeDtypeStruct((bsz, seq, CONV_WIDTH), BF16), hy_out, hy_out, hy_out],
        scratch_shapes=[pltpu.VMEM((CC_EXT, D_MODEL), BF16),
                        pltpu.VMEM((CC_EXT, CONV_WIDTH), F32),
                        pltpu.VMEM((SUBLANES, CC_SH_ROWS, CONV_WIDTH), F32),
                        pltpu.VMEM((CC_EXT, HYENA_IN), F32)],
        compiler_params=_params("parallel", "parallel"),
        name="front",
    )(x, x, x, gn, w_in, cv_w, cv_b, ln_g, ln_b, short_w, short_b)


FM_TL = 512
FEAT = 64
MASK_COL = HYENA_EMB_DIM
N_FILT = 2 * HYENA_WIDTH


@functools.lru_cache(maxsize=None)
def _filter_features(seq):
    t = np.linspace(0.0, 1.0, seq, dtype=np.float32).astype(np.float64)
    bands = (HYENA_EMB_DIM - 1) // 2
    w = (2.0 * math.pi * np.arange(seq, dtype=np.float32) / np.float32(seq)).astype(np.float64)
    f = np.linspace(1e-4, bands - 1, bands, dtype=np.float32).astype(np.float64)
    fw = w[:, None] * f[None, :]
    z = np.concatenate([t[:, None], np.cos(fw), -np.sin(fw)], axis=-1)
    fwd = np.zeros((seq, FEAT), np.float64)
    fwd[:, :HYENA_EMB_DIM] = z
    fwd[:, MASK_COL] = 1.0
    bwd = np.zeros((seq, FEAT), np.float64)
    bwd[1:, :HYENA_EMB_DIM] = z[:0:-1]
    bwd[1:, MASK_COL] = 1.0
    return np.concatenate([fwd, bwd], axis=-1).astype(np.float32)


def _filter_mlp_kernel(z_ref, w1_ref, b1_ref, w2_ref, b2_ref, w3_ref, b3_ref, w4_ref, fr_ref,
                       dec_ref, k_ref, ssq_ref):
    hp = lax.Precision.HIGHEST
    z = z_ref[...]
    h = jnp.sin(fr_ref[0:1, :] * (jnp.dot(z, w1_ref[...], precision=hp,
                                          preferred_element_type=F32) + b1_ref[...]))
    h = jnp.sin(fr_ref[1:2, :] * (jnp.dot(h, w2_ref[...], precision=hp,
                                          preferred_element_type=F32) + b2_ref[...]))
    h = jnp.sin(fr_ref[2:3, :] * (jnp.dot(h, w3_ref[...], precision=hp,
                                          preferred_element_type=F32) + b3_ref[...]))
    ssq = jnp.zeros((1, N_FILT), F32)
    hb = h.astype(BF16)
    for d in range(2):
        k = jnp.dot(hb, w4_ref[d], preferred_element_type=F32)
        t = z[:, d * FEAT:d * FEAT + 1]
        mask = z[:, d * FEAT + MASK_COL:d * FEAT + MASK_COL + 1]
        k = k * jnp.exp(-t * jnp.abs(dec_ref[d])) * mask
        k_ref[d] = k
        ssq = ssq + jnp.sum(k * k, axis=0, keepdims=True)

    @pl.when(pl.program_id(0) == 0)
    def _():
        ssq_ref[...] = jnp.zeros_like(ssq_ref)

    ssq_ref[...] += ssq


def _filter_mlp(seq, w1, b1, w2, b2, w3, b3, w4, freq, decay):
    fh = FILTER_HIDDEN

    def blockdiag(w):
        z = jnp.zeros_like(w)
        return jnp.concatenate([jnp.concatenate([w, z], 1), jnp.concatenate([z, w], 1)], 0)

    w1p = jnp.zeros((FEAT, fh), F32).at[:HYENA_EMB_DIM].set(w1)
    w4r = w4.reshape(fh, 2, 2, HYENA_WIDTH)
    zeros = jnp.zeros((fh, N_FILT), F32)
    w4f = jnp.concatenate([w4r[:, :, 0].reshape(fh, N_FILT), zeros], 0)
    w4b = jnp.concatenate([zeros, w4r[:, :, 1].reshape(fh, N_FILT)], 0)
    dec = decay.reshape(2, 2, HYENA_WIDTH)
    args = (
        jnp.asarray(_filter_features(seq)),
        blockdiag(w1p), jnp.tile(b1, 2)[None], blockdiag(w2), jnp.tile(b2, 2)[None],
        blockdiag(w3), jnp.tile(b3, 2)[None],
        jnp.stack([w4f, w4b]).astype(BF16),
        jnp.tile(freq, (1, 2)),
        jnp.stack([dec[:, 0].reshape(1, N_FILT), dec[:, 1].reshape(1, N_FILT)]),
    )
    in_specs = [pl.BlockSpec((FM_TL, 2 * FEAT), lambda i: (i, 0))]
    in_specs += [_resident(a.shape) for a in args[1:]]
    return pl.pallas_call(
        _filter_mlp_kernel,
        grid=(seq // FM_TL,),
        in_specs=in_specs,
        out_specs=[pl.BlockSpec((2, FM_TL, N_FILT), lambda i: (0, i, 0)),
                   pl.BlockSpec((1, N_FILT), lambda i: (0, 0))],
        out_shape=[jax.ShapeDtypeStruct((2, seq, N_FILT), F32),
                   jax.ShapeDtypeStruct((1, N_FILT), F32)],
        compiler_params=_params("arbitrary"),
        name="filter_mlp",
    )(*args)


@functools.lru_cache(maxsize=None)
def _dft_tables(q):
    n = DFT_P * q
    c = np.arange(DFT_P)[:, None]
    a = np.arange(DFT_P)[None, :]
    b = np.arange(q)[:, None, None]
    ang = -2.0 * np.pi * ((c * (q * a + b)) % n) / n
    fr, fi = np.cos(ang), np.sin(ang)
    d = np.arange(q)[:, None]
    ang2 = -2.0 * np.pi * ((d * np.arange(q)[None, :]) % q) / q
    gr, gi = np.cos(ang2), np.sin(ang2)
    g2 = np.block([[gr, -gi], [gi, gr]])
    g2i = np.block([[gr, gi], [-gi, gr]])
    return (fr.astype(np.float32), fi.astype(np.float32),
            g2.astype(np.float32), g2i.astype(np.float32))


def _dft_mats(q):
    fr, fi, g2, g2i = (jnp.asarray(t) for t in _dft_tables(q))
    frh, fih = fr[:, :, :DFT_PH], fi[:, :, :DFT_PH]
    m1 = jnp.concatenate([jnp.concatenate([frh, -fih], 2), jnp.concatenate([fih, frh], 2)], 1)
    m1f = jnp.concatenate([fr, fi], 1)
    hr, hi = jnp.swapaxes(frh, 1, 2), -jnp.swapaxes(fih, 1, 2)
    m3 = jnp.concatenate([jnp.concatenate([hr, -hi], 2), jnp.concatenate([hi, hr], 2)], 1)
    return (m1.astype(BF16), m1f.astype(BF16), m3.astype(BF16),
            g2.astype(BF16), g2i.astype(BF16))


S1_NB = 16
S1_CW = 256


def _dft_s1_kernel(x_ref, m_ref, ar_ref, ai_ref, re_buf, im_buf):
    x = x_ref[...].astype(BF16).reshape(DFT_P, S1_NB, S1_CW)
    xs = jnp.swapaxes(x, 0, 1)
    for j in range(S1_NB):
        r = jnp.dot(m_ref[j], xs[j], preferred_element_type=F32)
        re_buf[j] = r[:DFT_P].astype(BF16)
        im_buf[j] = r[DFT_P:].astype(BF16)
    ar_ref[...] = jnp.swapaxes(re_buf[...], 0, 1)
    ai_ref[...] = jnp.swapaxes(im_buf[...], 0, 1)


def _dft_stage1(x, m, q):
    g, wtot = x.shape[0] // 2, x.shape[3]
    out = jax.ShapeDtypeStruct((g, DFT_P, q, wtot), BF16)
    ospec = pl.BlockSpec((None, DFT_P, S1_NB, S1_CW), lambda gi, j, ci: (gi, 0, j, ci))
    stage = pltpu.VMEM((S1_NB, DFT_P, S1_CW), BF16)
    return pl.pallas_call(
        _dft_s1_kernel,
        grid=(g, q // S1_NB, wtot // S1_CW),
        in_specs=[pl.BlockSpec((2, DFT_PH, S1_NB, S1_CW), lambda gi, j, ci: (gi, 0, j, ci)),
                  pl.BlockSpec((S1_NB, 2 * DFT_P, DFT_P), lambda gi, j, ci: (j, 0, 0))],
        out_specs=[ospec, ospec],
        out_shape=[out, out],
        scratch_shapes=[stage, stage],
        compiler_params=_params("parallel", "parallel", "parallel"),
        name="dft_stage1",
    )(x, m)


S2_NC = 8


def _filter_s2_kernel(ar_ref, ai_ref, g2_ref, ssq_ref, kr_ref, ki_ref, *, q, inv_n):
    scale = lax.rsqrt(ssq_ref[...] + FILTER_EPS) * inv_n
    for c in range(S2_NC):
        a = jnp.concatenate([ar_ref[c], ai_ref[c]], axis=0)
        x = jnp.dot(g2_ref[...], a, preferred_element_type=F32)
        kr_ref[c] = x[:q] * scale
        ki_ref[c] = x[q:] * scale


def _filter_stage2(ar, ai, g2, ssq, q):
    spec = pl.BlockSpec((S2_NC, q, N_FILT), lambda i: (i, 0, 0))
    out = jax.ShapeDtypeStruct((DFT_P, q, N_FILT), F32)
    return pl.pallas_call(
        functools.partial(_filter_s2_kernel, q=q, inv_n=1.0 / (DFT_P * q)),
        grid=(DFT_P // S2_NC,),
        in_specs=[spec, spec, _resident((2 * q, 2 * q)), _resident((1, N_FILT))],
        out_specs=[spec, spec],
        out_shape=[out, out],
        compiler_params=_params("parallel"),
        name="filter_stage2",
    )(ar, ai, g2, ssq)


def _dft_mid_kernel(ar_ref, ai_ref, g2_ref, g2i_ref, kr_ref, ki_ref, yr_ref, yi_ref, *, q):
    for c in range(S2_NC):
        a = jnp.concatenate([ar_ref[c], ai_ref[c]], axis=0)
        x = jnp.dot(g2_ref[...], a, preferred_element_type=F32)
        xr, xi = x[:q], x[q:]
        kr, ki = kr_ref[c], ki_ref[c]
        z = jnp.concatenate([xr * kr - xi * ki, xr * ki + xi * kr], axis=0).astype(BF16)
        y = jnp.dot(g2i_ref[...], z, preferred_element_type=F32)
        yr_ref[c] = y[:q].astype(BF16)
        yi_ref[c] = y[q:].astype(BF16)


def _dft_mid(ar, ai, g2, g2i, kr, ki, order, q):
    pairs = ar.shape[0]
    cw = HYENA_WIDTH
    spec = pl.BlockSpec((None, S2_NC, q, cw), lambda i, p: (p, i, 0, 0))
    kspec = pl.BlockSpec((S2_NC, q, cw), lambda i, p: (i, 0, order))
    out = jax.ShapeDtypeStruct(ar.shape, BF16)
    return pl.pallas_call(
        functools.partial(_dft_mid_kernel, q=q),
        grid=(DFT_P // S2_NC, pairs),
        in_specs=[spec, spec, _resident((2 * q, 2 * q)), _resident((2 * q, 2 * q)), kspec, kspec],
        out_specs=[spec, spec],
        out_shape=[out, out],
        compiler_params=_params("parallel", "parallel"),
        name="dft_mid",
    )(ar, ai, g2, g2i, kr, ki)


def _dft_s3_kernel(yr_ref, yi_ref, m_ref, gate_ref, z_ref, sk_ref, o_ref, y_buf):
    yr = jnp.swapaxes(yr_ref[...], 0, 1)
    yi = jnp.swapaxes(yi_ref[...], 0, 1)
    for j in range(S1_NB):
        yc = jnp.concatenate([yr[j], yi[j]], axis=0)
        y_buf[j] = jnp.dot(m_ref[j], yc, preferred_element_type=F32)
    y = jnp.swapaxes(y_buf[...], 0, 1).reshape(2, DFT_PH, S1_NB, S1_CW)
    z = z_ref[...].astype(F32)
    o_ref[...] = (gate_ref[...].astype(F32) * (y + z * sk_ref[...])).astype(o_ref.dtype)


def _dft_stage3(yr, yi, m3, gate, z, sk, q):
    pairs, cw = yr.shape[0], yr.shape[3]
    yspec = pl.BlockSpec((None, DFT_P, S1_NB, S1_CW), lambda p, j, ci: (p, 0, j, ci))
    tspec = pl.BlockSpec((2, DFT_PH, S1_NB, S1_CW), lambda p, j, ci: (p, 0, j, ci))
    return pl.pallas_call(
        _dft_s3_kernel,
        grid=(pairs, q // S1_NB, cw // S1_CW),
        in_specs=[yspec, yspec,
                  pl.BlockSpec((S1_NB, DFT_P, 2 * DFT_P), lambda p, j, ci: (j, 0, 0)),
                  tspec, tspec, pl.BlockSpec((1, S1_CW), lambda p, j, ci: (0, ci))],
        out_specs=tspec,
        out_shape=jax.ShapeDtypeStruct(z.shape, BF16),
        scratch_shapes=[pltpu.VMEM((S1_NB, DFT_P, S1_CW), F32)],
        compiler_params=_params("parallel", "parallel", "parallel"),
        name="dft_stage3",
    )(yr, yi, m3, gate, z, sk)


def _hyena(x1, x2, v, w1, b1, w2, b2, w3, b3, w4, freq, decay, skip):
    bsz, seq, _ = v.shape
    q = 2 * seq // DFT_P
    cw = HYENA_WIDTH
    m1, m1f, m3, g2, g2i = _dft_mats(q)

    k_time, ssq = _filter_mlp(seq, w1, b1, w2, b2, w3, b3, w4, freq, decay)
    afr, afi = _dft_stage1(k_time.reshape(2, DFT_PH, q, N_FILT), m1f, q)
    kr, ki = _filter_stage2(afr.reshape(DFT_P, q, N_FILT), afi.reshape(DFT_P, q, N_FILT),
                            g2, ssq, q)

    time_view = (bsz, DFT_PH, q, cw)
    z = v.reshape(time_view)
    gates = (x1.reshape(time_view), x2.reshape(time_view))
    for n in range(2):
        ar, ai = _dft_stage1(z, m1, q)
        yr, yi = _dft_mid(ar, ai, g2, g2i, kr, ki, n, q)
        z = _dft_stage3(yr, yi, m3, gates[n], z, skip[n][None], q)
    return z.reshape(bsz, seq, cw)


PM_TM = 512
PM_FC = 1024


def _proj_mlp_kernel(*refs, n_parts, final):
    x_ref = refs[0]
    part_refs = refs[1:1 + n_parts]
    wo_ref, gm_ref, wup_ref, wdn_ref = refs[1 + n_parts:5 + n_parts]
    gf_ref = refs[5 + n_parts] if final else None
    o_ref = refs[-1]
    mixed = jnp.concatenate([p_ref[...] for p_ref in part_refs], axis=1)
    x = x_ref[...] + jnp.dot(mixed, wo_ref[...], preferred_element_type=F32)
    hn = _rms(x, gm_ref[...]).astype(BF16)
    for ch in range(D_FF // PM_FC):
        cs = slice(ch * PM_FC, (ch + 1) * PM_FC)
        h = jnp.dot(hn, wup_ref[:, cs], preferred_element_type=F32)
        h = jnp.square(jnp.maximum(h, 0.0)).astype(BF16)
        x = x + jnp.dot(h, wdn_ref[cs, :], preferred_element_type=F32)
    if final:
        x = _rms(x, gf_ref[...])
    o_ref[...] = x


def _proj_mlp(x, parts, wo, gm, wup, wdn, gf=None):
    t = x.shape[0]
    final = gf is not None
    row = lambda i: (i, 0)
    in_specs = [pl.BlockSpec((PM_TM, D_MODEL), row)]
    in_specs += [pl.BlockSpec((PM_TM, p.shape[1]), row) for p in parts]
    in_specs += [_resident(wo.shape), _resident((1, D_MODEL)), _resident(wup.shape),
                 _resident(wdn.shape)]
    args = [x, *parts, wo, gm, wup, wdn]
    if final:
        in_specs.append(_resident((1, D_MODEL)))
        args.append(gf)
    return pl.pallas_call(
        functools.partial(_proj_mlp_kernel, n_parts=len(parts), final=final),
        grid=(t // PM_TM,),
        in_specs=in_specs,
        out_specs=pl.BlockSpec((PM_TM, D_MODEL), row),
        out_shape=jax.ShapeDtypeStruct((t, D_MODEL), F32),
        compiler_params=_params("parallel"),
        name="proj_mlp",
    )(*args)


QK_TM = 512
Q_WIDTH = N_HEADS * HEAD_DIM
KV_WIDTH = N_KV_HEADS * HEAD_DIM
ROT_HALF = ROT_DIM // 2
LOG2E = math.log2(math.e)
Q_SCALE = HEAD_DIM ** -0.5 * LOG2E


@functools.lru_cache(maxsize=None)
def _rope_tables(seq):
    inv = ROPE_THETA ** (-(np.arange(0, ROT_DIM, 2, dtype=np.float64) / ROT_DIM))
    ang = np.arange(seq, dtype=np.float64)[:, None] * inv[None, :]
    cos = np.ones((seq, HEAD_DIM))
    s_lo = np.zeros((seq, HEAD_DIM))
    s_hi = np.zeros((seq, HEAD_DIM))
    cos[:, :ROT_HALF] = np.cos(ang)
    cos[:, ROT_HALF:ROT_DIM] = np.cos(ang)
    s_lo[:, ROT_HALF:ROT_DIM] = np.sin(ang)
    s_hi[:, :ROT_HALF] = -np.sin(ang)
    rep = LANES // HEAD_DIM
    return np.stack([np.tile(cos, (1, rep)), np.tile(s_lo, (1, rep)),
                     np.tile(s_hi, (1, rep))]).astype(np.float32)


def _qkv_kernel(x_ref, g_ref, w_ref, rope_ref, q_ref, k_ref, v_ref):
    hn = _rms(x_ref[...], g_ref[...]).astype(BF16)
    cos, s_lo, s_hi = rope_ref[0], rope_ref[1], rope_ref[2]
    nq = Q_WIDTH // LANES
    qk = jnp.dot(hn, w_ref[:, :Q_WIDTH + KV_WIDTH], preferred_element_type=F32)
    for blk in range((Q_WIDTH + KV_WIDTH) // LANES):
        t = qk[:, blk * LANES:(blk + 1) * LANES]
        r = (t * cos + pltpu.roll(t, ROT_HALF, axis=1) * s_lo
             + pltpu.roll(t, LANES - ROT_HALF, axis=1) * s_hi)
        if blk < nq:
            q_ref[:, blk * LANES:(blk + 1) * LANES] = (r * Q_SCALE).astype(BF16)
        else:
            k_ref[:, (blk - nq) * LANES:(blk - nq + 1) * LANES] = r.astype(BF16)
    v_ref[...] = jnp.dot(hn, w_ref[:, Q_WIDTH + KV_WIDTH:],
                         preferred_element_type=F32).astype(BF16)


def _qkv(x, g, w, seq):
    t = x.shape[0]
    per_seq = seq // QK_TM
    row = lambda i: (i, 0)
    return pl.pallas_call(
        _qkv_kernel,
        grid=(t // QK_TM,),
        in_specs=[pl.BlockSpec((QK_TM, D_MODEL), row), _resident((1, D_MODEL)),
                  _resident(w.shape),
                  pl.BlockSpec((3, QK_TM, LANES), lambda i: (0, i % per_seq, 0))],
        out_specs=[pl.BlockSpec((QK_TM, Q_WIDTH), row), pl.BlockSpec((QK_TM, KV_WIDTH), row),
                   pl.BlockSpec((QK_TM, KV_WIDTH), row)],
        out_shape=[jax.ShapeDtypeStruct((t, Q_WIDTH), BF16),
                   jax.ShapeDtypeStruct((t, KV_WIDTH), BF16),
                   jax.ShapeDtypeStruct((t, KV_WIDTH), BF16)],
        compiler_params=_params("parallel"),
        name="qkv_rope",
    )(x, g, w, jnp.asarray(_rope_tables(seq)))


AT_TQ = 512
AT_BLK = WINDOW
AT_KEYS = 3 * AT_BLK
AT_ROWS = AT_TQ + 2 * AT_BLK
AT_RC = 32
GROUP = N_HEADS // N_KV_HEADS


def _attn_kernel(sink_ref, q_ref, k_ref, kp_ref, kn_ref, v_ref, vp_ref, vn_ref, o_ref,
                 klo, khi, vlo, vhi, s_scr, p_scr, r_scr):
    i = pl.program_id(1)
    last = pl.num_programs(1) - 1
    nb = AT_TQ // AT_BLK

    for (prev, main, nxt), lo_ref, hi_ref in (((kp_ref, k_ref, kn_ref), klo, khi),
                                              ((vp_ref, v_ref, vn_ref), vlo, vhi)):
        for row0, ref in ((0, prev), (AT_BLK, main), (AT_BLK + AT_TQ, nxt)):
            nrows = ref.shape[0]
            low = lax.broadcasted_iota(jnp.int32, (nrows, LANES), 1) < HEAD_DIM
            for kvh in range(N_KV_HEADS):
                t = ref[:, (kvh // 2) * LANES:(kvh // 2 + 1) * LANES]
                zero = jnp.zeros_like(t)
                if kvh % 2 == 0:
                    lo = jnp.where(low, t, zero)
                    hi = jnp.concatenate([zero[:, :HEAD_DIM], t[:, :HEAD_DIM]], axis=1)
                else:
                    hi = jnp.where(low, zero, t)
                    lo = jnp.concatenate([t[:, HEAD_DIM:], zero[:, :HEAD_DIM]], axis=1)
                lo_ref[kvh, row0:row0 + nrows, :] = lo
                hi_ref[kvh, row0:row0 + nrows, :] = hi

    rows2 = 2 * AT_BLK
    qrow = lax.broadcasted_iota(jnp.int32, (AT_RC, AT_BLK), 0)
    kcol = lax.broadcasted_iota(jnp.int32, (AT_RC, AT_BLK), 1)
    low_lane = lax.broadcasted_iota(jnp.int32, (AT_RC, LANES), 1) < HEAD_DIM

    def body(jb, carry):
        r0 = pl.multiple_of(jb * AT_BLK, AT_BLK)
        off_prev = jnp.where((i == 0) & (jb == 0), AT_BLK, 0)
        off_next = jnp.where((i == last) & (jb == nb - 1), AT_BLK, 0)
        for kvh in range(N_KV_HEADS):
            col = kvh * GROUP * HEAD_DIM
            qs = jnp.concatenate([q_ref[pl.ds(r0, AT_BLK), col:col + LANES],
                                  q_ref[pl.ds(r0, AT_BLK), col + LANES:col + 2 * LANES]], axis=0)
            kc = jnp.concatenate([klo[kvh, pl.ds(r0, AT_KEYS), :],
                                  khi[kvh, pl.ds(r0, AT_KEYS), :]], axis=0)
            s_scr[kvh] = lax.dot_general(qs, kc, (((1,), (1,)), ((), ())),
                                         preferred_element_type=F32)
        for kvh in range(N_KV_HEADS):
            for rc in range(rows2 // AT_RC):
                rs = slice(rc * AT_RC, (rc + 1) * AT_RC)
                qr = qrow + (rc * AT_RC) % AT_BLK
                m_prev = kcol >= qr + off_prev
                m_next = kcol <= qr - off_next
                probs, rinv = [], []
                for half in range(2):
                    head = kvh * GROUP + half + (2 if rc * AT_RC >= AT_BLK else 0)
                    sink = sink_ref[head] * LOG2E
                    c0 = half * AT_KEYS
                    a = jnp.where(m_prev, s_scr[kvh, rs, c0:c0 + AT_BLK], -jnp.inf)
                    b = s_scr[kvh, rs, c0 + AT_BLK:c0 + 2 * AT_BLK]
                    c = jnp.where(m_next, s_scr[kvh, rs, c0 + 2 * AT_BLK:c0 + 3 * AT_BLK],
                                  -jnp.inf)
                    m = jnp.max(jnp.maximum(jnp.maximum(a, b), c), axis=-1, keepdims=True)
                    m = jnp.maximum(m, sink)
                    pa, pb, pc = jnp.exp2(a - m), jnp.exp2(b - m), jnp.exp2(c - m)
                    denom = jnp.sum(pa + pb + pc, axis=-1, keepdims=True) + jnp.exp2(sink - m)
                    probs += [pa, pb, pc]
                    rinv.append(1.0 / denom)
                p_scr[kvh, rs, :] = jnp.concatenate(probs, axis=1).astype(BF16)
                r_scr[kvh, rs, :] = jnp.where(low_lane, rinv[0], rinv[1])
        for kvh in range(N_KV_HEADS):
            col = kvh * GROUP * HEAD_DIM
            vc = jnp.concatenate([vlo[kvh, pl.ds(r0, AT_KEYS), :],
                                  vhi[kvh, pl.ds(r0, AT_KEYS), :]], axis=0)
            o = jnp.dot(p_scr[kvh], vc, preferred_element_type=F32)
            o = (o * r_scr[kvh]).astype(BF16)
            o_ref[pl.ds(r0, AT_BLK), col:col + LANES] = o[:AT_BLK]
            o_ref[pl.ds(r0, AT_BLK), col + LANES:col + 2 * LANES] = o[AT_BLK:]
        return carry

    lax.fori_loop(0, nb, body, 0)


def _attention(q, k, v, sink, bsz, seq):
    nb = AT_TQ // AT_BLK
    n_blk = seq // AT_BLK
    main = lambda w: pl.BlockSpec((None, AT_TQ, w), lambda b, i: (b, i, 0))
    prev = pl.BlockSpec((None, AT_BLK, KV_WIDTH), lambda b, i: (b, jnp.maximum(i * nb - 1, 0), 0))
    nxt = pl.BlockSpec((None, AT_BLK, KV_WIDTH),
                       lambda b, i: (b, jnp.minimum((i + 1) * nb, n_blk - 1), 0))
    q3 = q.reshape(bsz, seq, Q_WIDTH)
    k3 = k.reshape(bsz, seq, KV_WIDTH)
    v3 = v.reshape(bsz, seq, KV_WIDTH)
    head_copy = pltpu.VMEM((N_KV_HEADS, AT_ROWS, LANES), BF16)
    out = pl.pallas_call(
        _attn_kernel,
        grid=(bsz, seq // AT_TQ),
        in_specs=[pl.BlockSpec(memory_space=pltpu.SMEM),
                  main(Q_WIDTH), main(KV_WIDTH), prev, nxt, main(KV_WIDTH), prev, nxt],
        out_specs=main(Q_WIDTH),
        out_shape=jax.ShapeDtypeStruct((bsz, seq, Q_WIDTH), BF16),
        scratch_shapes=[head_copy, head_copy, head_copy, head_copy,
                        pltpu.VMEM((N_KV_HEADS, 2 * AT_BLK, 2 * AT_KEYS), F32),
                        pltpu.VMEM((N_KV_HEADS, 2 * AT_BLK, 2 * AT_KEYS), BF16),
                        pltpu.VMEM((N_KV_HEADS, 2 * AT_BLK, LANES), F32)],
        compiler_params=_params("parallel", "parallel"),
        name="window_attention",
    )(sink, q3, k3, k3, k3, v3, v3, v3)
    return out.reshape(bsz * seq, Q_WIDTH)


def _trunk(x, norm_mix, norm_mlp, norm_final, ab_w_in, ab_w_out, cv_dw_w, cv_dw_b, cv_ln_g,
           cv_ln_b, hy_short_w, hy_short_b, hy_w1, hy_b1, hy_w2, hy_b2, hy_w3, hy_b3, hy_w4,
           hy_freq, hy_decay, hy_skip, at_w_qkv, at_sink, at_w_o, mlp_w_up, mlp_w_down):
    bsz, seq, _ = x.shape
    t = bsz * seq
    x2 = x.reshape(t, D_MODEL)

    y_a, gate1, gate2, v = _front(x, norm_mix[0][None], ab_w_in[0].astype(BF16), cv_dw_w[0],
                                  cv_dw_b[0][None], cv_ln_g[0][None], cv_ln_b[0][None],
                                  hy_short_w[0], hy_short_b[0][None])
    y_b = _hyena(gate1, gate2, v, hy_w1[0], hy_b1[0], hy_w2[0], hy_b2[0], hy_w3[0], hy_b3[0],
                 hy_w4[0], hy_freq[0], hy_decay[0], hy_skip[0])
    x2 = _proj_mlp(x2, [y_a.reshape(t, CONV_WIDTH), y_b.reshape(t, HYENA_WIDTH)],
                   ab_w_out[0].astype(BF16), norm_mlp[0][None], mlp_w_up[0].astype(BF16),
                   mlp_w_down[0].astype(BF16))

    q, k, v = _qkv(x2, norm_mix[1][None], at_w_qkv[0].astype(BF16), seq)
    o = _attention(q, k, v, at_sink[0], bsz, seq)
    x2 = _proj_mlp(x2, [o], at_w_o[0].astype(BF16), norm_mlp[1][None],
                   mlp_w_up[1].astype(BF16), mlp_w_down[1].astype(BF16), gf=norm_final[None])
    return x2.reshape(bsz, seq, D_MODEL)


def kernel(x_prompt, x_sample, norm_mix, norm_mlp, norm_final, ab_w_in, ab_w_out, cv_dw_w, cv_dw_b,
           cv_ln_g, cv_ln_b, hy_short_w, hy_short_b, hy_w1, hy_b1, hy_w2, hy_b2, hy_w3, hy_b3, hy_w4,
           hy_freq, hy_decay, hy_skip, at_w_qkv, at_sink, at_w_o, mlp_w_up, mlp_w_down):
    weights = (norm_mix, norm_mlp, norm_final, ab_w_in, ab_w_out, cv_dw_w, cv_dw_b, cv_ln_g,
               cv_ln_b, hy_short_w, hy_short_b, hy_w1, hy_b1, hy_w2, hy_b2, hy_w3, hy_b3, hy_w4,
               hy_freq, hy_decay, hy_skip, at_w_qkv, at_sink, at_w_o, mlp_w_up, mlp_w_down)
    return (_trunk(x_prompt, *weights), _trunk(x_sample, *weights))
```

```python
import functools
import math

import numpy as np
import jax
import jax.numpy as jnp
from jax import lax
from jax.experimental import pallas as pl
from jax.experimental.pallas import tpu as pltpu

F32 = jnp.float32
BF16 = jnp.bfloat16

D_MODEL = 1024
CONV_WIDTH = 512
CONV_KERNEL = 31
HYENA_WIDTH = 512
HYENA_IN = 3 * HYENA_WIDTH
HYENA_EMB_DIM = 33
FILTER_HIDDEN = 64
N_HEADS = 16
N_KV_HEADS = 4
HEAD_DIM = 64
ROT_DIM = 16
ROPE_THETA = 500000.0
WINDOW = 128
D_FF = 4 * D_MODEL
NORM_EPS = 1e-5
LN_EPS = 1e-5
FILTER_EPS = 1e-6

LANES = 128
DFT_P = 128
DFT_PH = DFT_P // 2
VMEM_LIMIT = 56 * 1024 * 1024


def _params(*sem):
    return pltpu.CompilerParams(dimension_semantics=sem, vmem_limit_bytes=VMEM_LIMIT)


def _resident(shape):
    nd = len(shape)
    return pl.BlockSpec(shape, lambda *_: (0,) * nd, pipeline_mode=pl.Buffered(1))


def _rms(x, g):
    return x * lax.rsqrt(jnp.mean(x * x, axis=-1, keepdims=True) + NORM_EPS) * g


CC_TR = 512
CC_HALO = 16
CC_EXT = CC_TR + 2 * CC_HALO
CC_RC = 32
SUBLANES = 8
CC_SHIFT0 = CC_HALO - CONV_KERNEL // 2
CC_MAX_ALIGNED = (CC_SHIFT0 + CONV_KERNEL - 1) // SUBLANES * SUBLANES
CC_SH_ROWS = CC_TR + CC_MAX_ALIGNED


def _front_kernel(main_ref, prev_ref, next_ref, gn_ref, win_ref, w_ref, b_ref, g_ref, beta_ref,
                  sw_ref, sb_ref, o_ref, x1_ref, x2_ref, v_ref, hn_buf, hbuf, hsh, ubuf):
    i = pl.program_id(1)
    last = pl.num_programs(1) - 1
    gn = gn_ref[...]
    hn_buf[0:CC_HALO, :] = _rms(prev_ref[...], gn).astype(BF16)
    hn_buf[CC_HALO:CC_HALO + CC_TR, :] = _rms(main_ref[...], gn).astype(BF16)
    hn_buf[CC_HALO + CC_TR:, :] = _rms(next_ref[...], gn).astype(BF16)
    first_row = jnp.where(i > 0, 0, CC_HALO)
    end_row = jnp.where(i < last, CC_EXT, CC_HALO + CC_TR)
    row = lax.broadcasted_iota(jnp.int32, (CC_EXT, 1), 0)
    valid = (row >= first_row) & (row < end_row)

    uc = jnp.dot(hn_buf[...], win_ref[:, :2 * CONV_WIDTH], preferred_element_type=F32)
    hbuf[...] = jnp.where(valid, uc[:, :CONV_WIDTH] * jax.nn.sigmoid(uc[:, CONV_WIDTH:]), 0.0)
    uh = jnp.dot(hn_buf[...], win_ref[:, 2 * CONV_WIDTH:], preferred_element_type=F32)
    ubuf[...] = jnp.where(valid, uh, 0.0)

    for n, s_ref in enumerate((x1_ref, x2_ref, v_ref)):
        cs = slice(n * HYENA_WIDTH, (n + 1) * HYENA_WIDTH)
        s_ref[...] = (ubuf[CC_HALO - 1:CC_HALO - 1 + CC_TR, cs] * sw_ref[0:1, cs]
                      + ubuf[CC_HALO:CC_HALO + CC_TR, cs] * sw_ref[1:2, cs]
                      + ubuf[CC_HALO + 1:CC_HALO + 1 + CC_TR, cs] * sw_ref[2:3, cs]
                      + sb_ref[:, cs]).astype(s_ref.dtype)

    for ph in range(SUBLANES):
        hsh[ph] = hbuf[ph:ph + CC_SH_ROWS, :]
    for r in range(CC_TR // CC_RC):
        acc = jnp.broadcast_to(b_ref[...], (CC_RC, CONV_WIDTH))
        for j in range(CONV_KERNEL):
            ph, lo = (j + CC_SHIFT0) % SUBLANES, r * CC_RC + (j + CC_SHIFT0) // SUBLANES * SUBLANES
            tap = jnp.tile(w_ref[j], (CC_RC // SUBLANES, 1))
            acc = acc + hsh[ph, lo:lo + CC_RC, :] * tap
        mu = jnp.mean(acc, axis=-1, keepdims=True)
        d = acc - mu
        var = jnp.mean(d * d, axis=-1, keepdims=True)
        y = d * lax.rsqrt(var + LN_EPS) * g_ref[...] + beta_ref[...]
        o_ref[r * CC_RC:(r + 1) * CC_RC, :] = (y * jax.nn.sigmoid(y)).astype(o_ref.dtype)


def _front(x, gn, w_in, cv_w, cv_b, ln_g, ln_b, short_w, short_b):
    bsz, seq, _ = x.shape
    nh = CC_TR // CC_HALO
    n_halo = seq // CC_HALO
    tile = lambda w: pl.BlockSpec((None, CC_TR, w), lambda bi, i: (bi, i, 0))
    hy_out = jax.ShapeDtypeStruct((bsz, seq, HYENA_WIDTH), BF16)
    return pl.pallas_call(
        _front_kernel,
        grid=(bsz, seq // CC_TR),
        in_specs=[
            tile(D_MODEL),
            pl.BlockSpec((None, CC_HALO, D_MODEL),
                         lambda bi, i: (bi, jnp.maximum(i * nh - 1, 0), 0)),
            pl.BlockSpec((None, CC_HALO, D_MODEL),
                         lambda bi, i: (bi, jnp.minimum((i + 1) * nh, n_halo - 1), 0)),
            _resident((1, D_MODEL)), _resident(w_in.shape),
            _resident((CONV_KERNEL, SUBLANES, CONV_WIDTH)), _resident((1, CONV_WIDTH)),
            _resident((1, CONV_WIDTH)), _resident((1, CONV_WIDTH)),
            _resident((3, HYENA_IN)), _resident((1, HYENA_IN)),
        ],
        out_specs=[tile(CONV_WIDTH), tile(HYENA_WIDTH), tile(HYENA_WIDTH), tile(HYENA_WIDTH)],
        out_shape=[jax.ShapeDtypeStruct((bsz, seq, CONV_WIDTH), BF16), hy_out, hy_out, hy_out],
        scratch_shapes=[pltpu.VMEM((CC_EXT, D_MODEL), BF16),
                        pltpu.VMEM((CC_EXT, CONV_WIDTH), F32),
                        pltpu.VMEM((SUBLANES, CC_SH_ROWS, CONV_WIDTH), F32),
                        pltpu.VMEM((CC_EXT, HYENA_IN), F32)],
        compiler_params=_params("parallel", "parallel"),
        name="front",
    )(x, x, x, gn, w_in,
      jnp.broadcast_to(cv_w[:, None, :], (CONV_KERNEL, SUBLANES, CONV_WIDTH)),
      cv_b, ln_g, ln_b, short_w, short_b)


FM_TL = 512
FEAT = 64
MASK_COL = HYENA_EMB_DIM
N_FILT = 2 * HYENA_WIDTH


@functools.lru_cache(maxsize=None)
def _filter_features(seq):
    t = np.linspace(0.0, 1.0, seq, dtype=np.float32).astype(np.float64)
    bands = (HYENA_EMB_DIM - 1) // 2
    w = (2.0 * math.pi * np.arange(seq, dtype=np.float32) / np.float32(seq)).astype(np.float64)
    f = np.linspace(1e-4, bands - 1, bands, dtype=np.float32).astype(np.float64)
    fw = w[:, None] * f[None, :]
    z = np.concatenate([t[:, None], np.cos(fw), -np.sin(fw)], axis=-1)
    fwd = np.zeros((seq, FEAT), np.float64)
    fwd[:, :HYENA_EMB_DIM] = z
    fwd[:, MASK_COL] = 1.0
    bwd = np.zeros((seq, FEAT), np.float64)
    bwd[1:, :HYENA_EMB_DIM] = z[:0:-1]
    bwd[1:, MASK_COL] = 1.0
    return np.concatenate([fwd, bwd], axis=-1).astype(np.float32)


def _filter_mlp_kernel(z_ref, w1_ref, b1_ref, w2_ref, b2_ref, w3_ref, b3_ref, w4_ref, fr_ref,
                       dec_ref, k_ref, ssq_ref):
    hp = lax.Precision.HIGHEST
    z = z_ref[...]
    h = jnp.sin(fr_ref[0:1, :] * (jnp.dot(z, w1_ref[...], precision=hp,
                                          preferred_element_type=F32) + b1_ref[...]))
    h = jnp.sin(fr_ref[1:2, :] * (jnp.dot(h, w2_ref[...], precision=hp,
                                          preferred_element_type=F32) + b2_ref[...]))
    h = jnp.sin(fr_ref[2:3, :] * (jnp.dot(h, w3_ref[...], precision=hp,
                                          preferred_element_type=F32) + b3_ref[...]))
    ssq = jnp.zeros((1, N_FILT), F32)
    hb = h.astype(BF16)
    for d in range(2):
        k = jnp.dot(hb, w4_ref[d], preferred_element_type=F32)
        t = z[:, d * FEAT:d * FEAT + 1]
        mask = z[:, d * FEAT + MASK_COL:d * FEAT + MASK_COL + 1]
        k = k * jnp.exp(-t * jnp.abs(dec_ref[d])) * mask
        k_ref[d] = k
        ssq = ssq + jnp.sum(k * k, axis=0, keepdims=True)

    @pl.when(pl.program_id(0) == 0)
    def _():
        ssq_ref[...] = jnp.zeros_like(ssq_ref)

    ssq_ref[...] += ssq


def _filter_mlp(seq, w1, b1, w2, b2, w3, b3, w4, freq, decay):
    fh = FILTER_HIDDEN

    def blockdiag(w):
        z = jnp.zeros_like(w)
        return jnp.concatenate([jnp.concatenate([w, z], 1), jnp.concatenate([z, w], 1)], 0)

    w1p = jnp.zeros((FEAT, fh), F32).at[:HYENA_EMB_DIM].set(w1)
    w4r = w4.reshape(fh, 2, 2, HYENA_WIDTH)
    zeros = jnp.zeros((fh, N_FILT), F32)
    w4f = jnp.concatenate([w4r[:, :, 0].reshape(fh, N_FILT), zeros], 0)
    w4b = jnp.concatenate([zeros, w4r[:, :, 1].reshape(fh, N_FILT)], 0)
    dec = decay.reshape(2, 2, HYENA_WIDTH)
    args = (
        jnp.asarray(_filter_features(seq)),
        blockdiag(w1p), jnp.tile(b1, 2)[None], blockdiag(w2), jnp.tile(b2, 2)[None],
        blockdiag(w3), jnp.tile(b3, 2)[None],
        jnp.stack([w4f, w4b]).astype(BF16),
        jnp.tile(freq, (1, 2)),
        jnp.stack([dec[:, 0].reshape(1, N_FILT), dec[:, 1].reshape(1, N_FILT)]),
    )
    in_specs = [pl.BlockSpec((FM_TL, 2 * FEAT), lambda i: (i, 0))]
    in_specs += [_resident(a.shape) for a in args[1:]]
    return pl.pallas_call(
        _filter_mlp_kernel,
        grid=(seq // FM_TL,),
        in_specs=in_specs,
        out_specs=[pl.BlockSpec((2, FM_TL, N_FILT), lambda i: (0, i, 0)),
                   pl.BlockSpec((1, N_FILT), lambda i: (0, 0))],
        out_shape=[jax.ShapeDtypeStruct((2, seq, N_FILT), F32),
                   jax.ShapeDtypeStruct((1, N_FILT), F32)],
        compiler_params=_params("arbitrary"),
        name="filter_mlp",
    )(*args)


@functools.lru_cache(maxsize=None)
def _dft_tables(q):
    n = DFT_P * q
    c = np.arange(DFT_P)[:, None]
    a = np.arange(DFT_P)[None, :]
    b = np.arange(q)[:, None, None]
    ang = -2.0 * np.pi * ((c * (q * a + b)) % n) / n
    fr, fi = np.cos(ang), np.sin(ang)
    d = np.arange(q)[:, None]
    ang2 = -2.0 * np.pi * ((d * np.arange(q)[None, :]) % q) / q
    gr, gi = np.cos(ang2), np.sin(ang2)
    g2 = np.block([[gr, -gi], [gi, gr]])
    g2i = np.block([[gr, gi], [-gi, gr]])
    return (fr.astype(np.float32), fi.astype(np.float32),
            g2.astype(np.float32), g2i.astype(np.float32))


def _dft_mats(q):
    fr, fi, g2, g2i = (jnp.asarray(t) for t in _dft_tables(q))
    frh, fih = fr[:, :, :DFT_PH], fi[:, :, :DFT_PH]
    m1 = jnp.concatenate([jnp.concatenate([frh, -fih], 2), jnp.concatenate([fih, frh], 2)], 1)
    m1f = jnp.concatenate([fr, fi], 1)
    hr, hi = jnp.swapaxes(frh, 1, 2), -jnp.swapaxes(fih, 1, 2)
    m3 = jnp.concatenate([jnp.concatenate([hr, -hi], 2), jnp.concatenate([hi, hr], 2)], 1)
    return (m1.astype(BF16), m1f.astype(BF16), m3.astype(BF16),
            g2.astype(BF16), g2i.astype(BF16))


S1_NB = 16
S1_CW = 256


def _dft_s1_kernel(x_ref, m_ref, ar_ref, ai_ref, re_buf, im_buf):
    x = x_ref[...].astype(BF16).reshape(DFT_P, S1_NB, S1_CW)
    xs = jnp.swapaxes(x, 0, 1)
    for j in range(S1_NB):
        r = jnp.dot(m_ref[j], xs[j], preferred_element_type=F32)
        re_buf[j] = r[:DFT_P].astype(BF16)
        im_buf[j] = r[DFT_P:].astype(BF16)
    ar_ref[...] = jnp.swapaxes(re_buf[...], 0, 1)
    ai_ref[...] = jnp.swapaxes(im_buf[...], 0, 1)


def _dft_stage1(x, m, q):
    g, wtot = x.shape[0] // 2, x.shape[3]
    out = jax.ShapeDtypeStruct((g, DFT_P, q, wtot), BF16)
    ospec = pl.BlockSpec((None, DFT_P, S1_NB, S1_CW), lambda gi, j, ci: (gi, 0, j, ci))
    stage = pltpu.VMEM((S1_NB, DFT_P, S1_CW), BF16)
    return pl.pallas_call(
        _dft_s1_kernel,
        grid=(g, q // S1_NB, wtot // S1_CW),
        in_specs=[pl.BlockSpec((2, DFT_PH, S1_NB, S1_CW), lambda gi, j, ci: (gi, 0, j, ci)),
                  pl.BlockSpec((S1_NB, 2 * DFT_P, DFT_P), lambda gi, j, ci: (j, 0, 0))],
        out_specs=[ospec, ospec],
        out_shape=[out, out],
        scratch_shapes=[stage, stage],
        compiler_params=_params("parallel", "parallel", "parallel"),
        name="dft_stage1",
    )(x, m)


S2_NC = 8
S2_RC = 16


def _filter_s2_kernel(ar_ref, ai_ref, g2_ref, ssq_ref, kr_ref, ki_ref, *, q, inv_n):
    scale = lax.rsqrt(ssq_ref[...] + FILTER_EPS) * inv_n
    for c in range(S2_NC):
        a = jnp.concatenate([ar_ref[c], ai_ref[c]], axis=0)
        x = jnp.dot(g2_ref[...], a, preferred_element_type=F32)
        kr_ref[c] = x[:q] * scale
        ki_ref[c] = x[q:] * scale


def _filter_stage2(ar, ai, g2, ssq, q):
    spec = pl.BlockSpec((S2_NC, q, N_FILT), lambda i: (i, 0, 0))
    out = jax.ShapeDtypeStruct((DFT_P, q, N_FILT), F32)
    return pl.pallas_call(
        functools.partial(_filter_s2_kernel, q=q, inv_n=1.0 / (DFT_P * q)),
        grid=(DFT_P // S2_NC,),
        in_specs=[spec, spec, _resident((2 * q, 2 * q)), _resident((1, N_FILT))],
        out_specs=[spec, spec],
        out_shape=[out, out],
        compiler_params=_params("parallel"),
        name="filter_stage2",
    )(ar, ai, g2, ssq)


def _dft_mid_kernel(ar_ref, ai_ref, g2_ref, g2i_ref, kr_ref, ki_ref, yr_ref, yi_ref,
                    x_scr, z_scr, *, q):
    for c in range(S2_NC):
        a = jnp.concatenate([ar_ref[c], ai_ref[c]], axis=0)
        x_scr[c] = jnp.dot(g2_ref[...], a, preferred_element_type=F32)
    for c in range(S2_NC):
        for r in range(q // S2_RC):
            re, im = slice(r * S2_RC, (r + 1) * S2_RC), slice(q + r * S2_RC, q + (r + 1) * S2_RC)
            xr, xi = x_scr[c, re, :], x_scr[c, im, :]
            kr, ki = kr_ref[c, re, :], ki_ref[c, re, :]
            z_scr[c, re, :] = (xr * kr - xi * ki).astype(BF16)
            z_scr[c, im, :] = (xr * ki + xi * kr).astype(BF16)
    for c in range(S2_NC):
        y = jnp.dot(g2i_ref[...], z_scr[c], preferred_element_type=F32)
        yr_ref[c] = y[:q].astype(BF16)
        yi_ref[c] = y[q:].astype(BF16)


def _dft_mid(ar, ai, g2, g2i, kr, ki, order, q):
    pairs = ar.shape[0]
    cw = HYENA_WIDTH
    spec = pl.BlockSpec((None, S2_NC, q, cw), lambda i, p: (p, i, 0, 0))
    kspec = pl.BlockSpec((S2_NC, q, cw), lambda i, p: (i, 0, order))
    out = jax.ShapeDtypeStruct(ar.shape, BF16)
    return pl.pallas_call(
        functools.partial(_dft_mid_kernel, q=q),
        grid=(DFT_P // S2_NC, pairs),
        in_specs=[spec, spec, _resident((2 * q, 2 * q)), _resident((2 * q, 2 * q)), kspec, kspec],
        out_specs=[spec, spec],
        out_shape=[out, out],
        scratch_shapes=[pltpu.VMEM((S2_NC, 2 * q, cw), F32), pltpu.VMEM((S2_NC, 2 * q, cw), BF16)],
        compiler_params=_params("parallel", "parallel"),
        name="dft_mid",
    )(ar, ai, g2, g2i, kr, ki)


def _dft_s3_kernel(yr_ref, yi_ref, m_ref, gate_ref, z_ref, sk_ref, o_ref, y_buf):
    yr = jnp.swapaxes(yr_ref[...], 0, 1)
    yi = jnp.swapaxes(yi_ref[...], 0, 1)
    for j in range(S1_NB):
        yc = jnp.concatenate([yr[j], yi[j]], axis=0)
        y_buf[j] = jnp.dot(m_ref[j], yc, preferred_element_type=F32)
    y = jnp.swapaxes(y_buf[...], 0, 1).reshape(2, DFT_PH, S1_NB, S1_CW)
    z = z_ref[...].astype(F32)
    o_ref[...] = (gate_ref[...].astype(F32) * (y + z * sk_ref[...])).astype(o_ref.dtype)


def _dft_stage3(yr, yi, m3, gate, z, sk, q):
    pairs, cw = yr.shape[0], yr.shape[3]
    yspec = pl.BlockSpec((None, DFT_P, S1_NB, S1_CW), lambda p, j, ci: (p, 0, j, ci))
    tspec = pl.BlockSpec((2, DFT_PH, S1_NB, S1_CW), lambda p, j, ci: (p, 0, j, ci))
    return pl.pallas_call(
        _dft_s3_kernel,
        grid=(pairs, q // S1_NB, cw // S1_CW),
        in_specs=[yspec, yspec,
                  pl.BlockSpec((S1_NB, DFT_P, 2 * DFT_P), lambda p, j, ci: (j, 0, 0)),
                  tspec, tspec, pl.BlockSpec((1, S1_CW), lambda p, j, ci: (0, ci))],
        out_specs=tspec,
        out_shape=jax.ShapeDtypeStruct(z.shape, BF16),
        scratch_shapes=[pltpu.VMEM((S1_NB, DFT_P, S1_CW), F32)],
        compiler_params=_params("parallel", "parallel", "parallel"),
        name="dft_stage3",
    )(yr, yi, m3, gate, z, sk)


def _hyena(x1, x2, v, w1, b1, w2, b2, w3, b3, w4, freq, decay, skip):
    bsz, seq, _ = v.shape
    q = 2 * seq // DFT_P
    cw = HYENA_WIDTH
    m1, m1f, m3, g2, g2i = _dft_mats(q)

    k_time, ssq = _filter_mlp(seq, w1, b1, w2, b2, w3, b3, w4, freq, decay)
    afr, afi = _dft_stage1(k_time.reshape(2, DFT_PH, q, N_FILT), m1f, q)
    kr, ki = _filter_stage2(afr.reshape(DFT_P, q, N_FILT), afi.reshape(DFT_P, q, N_FILT),
                            g2, ssq, q)

    time_view = (bsz, DFT_PH, q, cw)
    z = v.reshape(time_view)
    gates = (x1.reshape(time_view), x2.reshape(time_view))
    for n in range(2):
        ar, ai = _dft_stage1(z, m1, q)
        yr, yi = _dft_mid(ar, ai, g2, g2i, kr, ki, n, q)
        z = _dft_stage3(yr, yi, m3, gates[n], z, skip[n][None], q)
    return z.reshape(bsz, seq, cw)


PM_TM = 512
PM_FC = 1024


def _proj_mlp_kernel(*refs, n_parts, final):
    x_ref = refs[0]
    part_refs = refs[1:1 + n_parts]
    wo_ref, gm_ref, wup_ref, wdn_ref = refs[1 + n_parts:5 + n_parts]
    gf_ref = refs[5 + n_parts] if final else None
    o_ref = refs[-1]
    mixed = jnp.concatenate([p_ref[...] for p_ref in part_refs], axis=1)
    x = x_ref[...] + jnp.dot(mixed, wo_ref[...], preferred_element_type=F32)
    hn = _rms(x, gm_ref[...]).astype(BF16)
    for ch in range(D_FF // PM_FC):
        cs = slice(ch * PM_FC, (ch + 1) * PM_FC)
        h = jnp.dot(hn, wup_ref[:, cs], preferred_element_type=F32)
        h = jnp.square(jnp.maximum(h, 0.0)).astype(BF16)
        x = x + jnp.dot(h, wdn_ref[cs, :], preferred_element_type=F32)
    if final:
        x = _rms(x, gf_ref[...])
    o_ref[...] = x


def _proj_mlp(x, parts, wo, gm, wup, wdn, gf=None):
    t = x.shape[0]
    final = gf is not None
    row = lambda i: (i, 0)
    in_specs = [pl.BlockSpec((PM_TM, D_MODEL), row)]
    in_specs += [pl.BlockSpec((PM_TM, p.shape[1]), row) for p in parts]
    in_specs += [_resident(wo.shape), _resident((1, D_MODEL)), _resident(wup.shape),
                 _resident(wdn.shape)]
    args = [x, *parts, wo, gm, wup, wdn]
    if final:
        in_specs.append(_resident((1, D_MODEL)))
        args.append(gf)
    return pl.pallas_call(
        functools.partial(_proj_mlp_kernel, n_parts=len(parts), final=final),
        grid=(t // PM_TM,),
        in_specs=in_specs,
        out_specs=pl.BlockSpec((PM_TM, D_MODEL), row),
        out_shape=jax.ShapeDtypeStruct((t, D_MODEL), F32),
        compiler_params=_params("parallel"),
        name="proj_mlp",
    )(*args)


QK_TM = 512
Q_WIDTH = N_HEADS * HEAD_DIM
KV_WIDTH = N_KV_HEADS * HEAD_DIM
ROT_HALF = ROT_DIM // 2
LOG2E = math.log2(math.e)
Q_SCALE = HEAD_DIM ** -0.5 * LOG2E


@functools.lru_cache(maxsize=None)
def _rope_tables(seq):
    inv = ROPE_THETA ** (-(np.arange(0, ROT_DIM, 2, dtype=np.float64) / ROT_DIM))
    ang = np.arange(seq, dtype=np.float64)[:, None] * inv[None, :]
    cos = np.ones((seq, HEAD_DIM))
    s_lo = np.zeros((seq, HEAD_DIM))
    s_hi = np.zeros((seq, HEAD_DIM))
    cos[:, :ROT_HALF] = np.cos(ang)
    cos[:, ROT_HALF:ROT_DIM] = np.cos(ang)
    s_lo[:, ROT_HALF:ROT_DIM] = np.sin(ang)
    s_hi[:, :ROT_HALF] = -np.sin(ang)
    rep = LANES // HEAD_DIM
    return np.stack([np.tile(cos, (1, rep)), np.tile(s_lo, (1, rep)),
                     np.tile(s_hi, (1, rep))]).astype(np.float32)


def _qkv_kernel(x_ref, g_ref, w_ref, rope_ref, q_ref, k_ref, v_ref):
    hn = _rms(x_ref[...], g_ref[...]).astype(BF16)
    cos, s_lo, s_hi = rope_ref[0], rope_ref[1], rope_ref[2]
    nq = Q_WIDTH // LANES
    qk = jnp.dot(hn, w_ref[:, :Q_WIDTH + KV_WIDTH], preferred_element_type=F32)
    for blk in range((Q_WIDTH + KV_WIDTH) // LANES):
        t = qk[:, blk * LANES:(blk + 1) * LANES]
        r = (t * cos + pltpu.roll(t, ROT_HALF, axis=1) * s_lo
             + pltpu.roll(t, LANES - ROT_HALF, axis=1) * s_hi)
        if blk < nq:
            q_ref[:, blk * LANES:(blk + 1) * LANES] = (r * Q_SCALE).astype(BF16)
        else:
            k_ref[:, (blk - nq) * LANES:(blk - nq + 1) * LANES] = r.astype(BF16)
    v_ref[...] = jnp.dot(hn, w_ref[:, Q_WIDTH + KV_WIDTH:],
                         preferred_element_type=F32).astype(BF16)


def _qkv(x, g, w, seq):
    t = x.shape[0]
    per_seq = seq // QK_TM
    row = lambda i: (i, 0)
    return pl.pallas_call(
        _qkv_kernel,
        grid=(t // QK_TM,),
        in_specs=[pl.BlockSpec((QK_TM, D_MODEL), row), _resident((1, D_MODEL)),
                  _resident(w.shape),
                  pl.BlockSpec((3, QK_TM, LANES), lambda i: (0, i % per_seq, 0))],
        out_specs=[pl.BlockSpec((QK_TM, Q_WIDTH), row), pl.BlockSpec((QK_TM, KV_WIDTH), row),
                   pl.BlockSpec((QK_TM, KV_WIDTH), row)],
        out_shape=[jax.ShapeDtypeStruct((t, Q_WIDTH), BF16),
                   jax.ShapeDtypeStruct((t, KV_WIDTH), BF16),
                   jax.ShapeDtypeStruct((t, KV_WIDTH), BF16)],
        compiler_params=_params("parallel"),
        name="qkv_rope",
    )(x, g, w, jnp.asarray(_rope_tables(seq)))


AT_TQ = 512
AT_BLK = WINDOW
AT_KEYS = 3 * AT_BLK
AT_ROWS = AT_TQ + 2 * AT_BLK
AT_RC = 32
GROUP = N_HEADS // N_KV_HEADS


def _attn_kernel(sink_ref, q_ref, k_ref, kp_ref, kn_ref, v_ref, vp_ref, vn_ref, o_ref,
                 klo, khi, vlo, vhi, s_scr, p_scr, r_scr):
    i = pl.program_id(1)
    last = pl.num_programs(1) - 1
    nb = AT_TQ // AT_BLK

    for (prev, main, nxt), lo_ref, hi_ref in (((kp_ref, k_ref, kn_ref), klo, khi),
                                              ((vp_ref, v_ref, vn_ref), vlo, vhi)):
        for row0, ref in ((0, prev), (AT_BLK, main), (AT_BLK + AT_TQ, nxt)):
            nrows = ref.shape[0]
            low = lax.broadcasted_iota(jnp.int32, (nrows, LANES), 1) < HEAD_DIM
            for kvh in range(N_KV_HEADS):
                t = ref[:, (kvh // 2) * LANES:(kvh // 2 + 1) * LANES]
                zero = jnp.zeros_like(t)
                if kvh % 2 == 0:
                    lo = jnp.where(low, t, zero)
                    hi = jnp.concatenate([zero[:, :HEAD_DIM], t[:, :HEAD_DIM]], axis=1)
                else:
                    hi = jnp.where(low, zero, t)
                    lo = jnp.concatenate([t[:, HEAD_DIM:], zero[:, :HEAD_DIM]], axis=1)
                lo_ref[kvh, row0:row0 + nrows, :] = lo
                hi_ref[kvh, row0:row0 + nrows, :] = hi

    rows2 = 2 * AT_BLK
    qrow = lax.broadcasted_iota(jnp.int32, (AT_RC, AT_BLK), 0)
    kcol = lax.broadcasted_iota(jnp.int32, (AT_RC, AT_BLK), 1)
    low_lane = lax.broadcasted_iota(jnp.int32, (AT_RC, LANES), 1) < HEAD_DIM

    def body(jb, carry):
        r0 = pl.multiple_of(jb * AT_BLK, AT_BLK)
        off_prev = jnp.where((i == 0) & (jb == 0), AT_BLK, 0)
        off_next = jnp.where((i == last) & (jb == nb - 1), AT_BLK, 0)
        for kvh in range(N_KV_HEADS):
            col = kvh * GROUP * HEAD_DIM
            qs = jnp.concatenate([q_ref[pl.ds(r0, AT_BLK), col:col + LANES],
                                  q_ref[pl.ds(r0, AT_BLK), col + LANES:col + 2 * LANES]], axis=0)
            kc = jnp.concatenate([klo[kvh, pl.ds(r0, AT_KEYS), :],
                                  khi[kvh, pl.ds(r0, AT_KEYS), :]], axis=0)
            s_scr[kvh] = lax.dot_general(qs, kc, (((1,), (1,)), ((), ())),
                                         preferred_element_type=F32)
        for kvh in range(N_KV_HEADS):
            for rc in range(rows2 // AT_RC):
                rs = slice(rc * AT_RC, (rc + 1) * AT_RC)
                qr = qrow + (rc * AT_RC) % AT_BLK
                m_prev = kcol >= qr + off_prev
                m_next = kcol <= qr - off_next
                probs, rinv = [], []
                for half in range(2):
                    head = kvh * GROUP + half + (2 if rc * AT_RC >= AT_BLK else 0)
                    sink = sink_ref[head] * LOG2E
                    c0 = half * AT_KEYS
                    a = jnp.where(m_prev, s_scr[kvh, rs, c0:c0 + AT_BLK], -jnp.inf)
                    b = s_scr[kvh, rs, c0 + AT_BLK:c0 + 2 * AT_BLK]
                    c = jnp.where(m_next, s_scr[kvh, rs, c0 + 2 * AT_BLK:c0 + 3 * AT_BLK],
                                  -jnp.inf)
                    m = jnp.max(jnp.maximum(jnp.maximum(a, b), c), axis=-1, keepdims=True)
                    m = jnp.maximum(m, sink)
                    pa, pb, pc = jnp.exp2(a - m), jnp.exp2(b - m), jnp.exp2(c - m)
                    denom = jnp.sum(pa + pb + pc, axis=-1, keepdims=True) + jnp.exp2(sink - m)
                    probs += [pa, pb, pc]
                    rinv.append(1.0 / denom)
                p_scr[kvh, rs, :] = jnp.concatenate(probs, axis=1).astype(BF16)
                r_scr[kvh, rs, :] = jnp.where(low_lane, rinv[0], rinv[1])
        for kvh in range(N_KV_HEADS):
            col = kvh * GROUP * HEAD_DIM
            vc = jnp.concatenate([vlo[kvh, pl.ds(r0, AT_KEYS), :],
                                  vhi[kvh, pl.ds(r0, AT_KEYS), :]], axis=0)
            o = jnp.dot(p_scr[kvh], vc, preferred_element_type=F32)
            o = (o * r_scr[kvh]).astype(BF16)
            o_ref[pl.ds(r0, AT_BLK), col:col + LANES] = o[:AT_BLK]
            o_ref[pl.ds(r0, AT_BLK), col + LANES:col + 2 * LANES] = o[AT_BLK:]
        return carry

    lax.fori_loop(0, nb, body, 0)


def _attention(q, k, v, sink, bsz, seq):
    nb = AT_TQ // AT_BLK
    n_blk = seq // AT_BLK
    main = lambda w: pl.BlockSpec((None, AT_TQ, w), lambda b, i: (b, i, 0))
    prev = pl.BlockSpec((None, AT_BLK, KV_WIDTH), lambda b, i: (b, jnp.maximum(i * nb - 1, 0), 0))
    nxt = pl.BlockSpec((None, AT_BLK, KV_WIDTH),
                       lambda b, i: (b, jnp.minimum((i + 1) * nb, n_blk - 1), 0))
    q3 = q.reshape(bsz, seq, Q_WIDTH)
    k3 = k.reshape(bsz, seq, KV_WIDTH)
    v3 = v.reshape(bsz, seq, KV_WIDTH)
    head_copy = pltpu.VMEM((N_KV_HEADS, AT_ROWS, LANES), BF16)
    out = pl.pallas_call(
        _attn_kernel,
        grid=(bsz, seq // AT_TQ),
        in_specs=[pl.BlockSpec(memory_space=pltpu.SMEM),
                  main(Q_WIDTH), main(KV_WIDTH), prev, nxt, main(KV_WIDTH), prev, nxt],
        out_specs=main(Q_WIDTH),
        out_shape=jax.ShapeDtypeStruct((bsz, seq, Q_WIDTH), BF16),
        scratch_shapes=[head_copy, head_copy, head_copy, head_copy,
                        pltpu.VMEM((N_KV_HEADS, 2 * AT_BLK, 2 * AT_KEYS), F32),
                        pltpu.VMEM((N_KV_HEADS, 2 * AT_BLK, 2 * AT_KEYS), BF16),
                        pltpu.VMEM((N_KV_HEADS, 2 * AT_BLK, LANES), F32)],
        compiler_params=_params("parallel", "parallel"),
        name="window_attention",
    )(sink, q3, k3, k3, k3, v3, v3, v3)
    return out.reshape(bsz * seq, Q_WIDTH)


def _trunk(x, norm_mix, norm_mlp, norm_final, ab_w_in, ab_w_out, cv_dw_w, cv_dw_b, cv_ln_g,
           cv_ln_b, hy_short_w, hy_short_b, hy_w1, hy_b1, hy_w2, hy_b2, hy_w3, hy_b3, hy_w4,
           hy_freq, hy_decay, hy_skip, at_w_qkv, at_sink, at_w_o, mlp_w_up, mlp_w_down):
    bsz, seq, _ = x.shape
    t = bsz * seq
    x2 = x.reshape(t, D_MODEL)

    y_a, gate1, gate2, v = _front(x, norm_mix[0][None], ab_w_in[0].astype(BF16), cv_dw_w[0],
                                  cv_dw_b[0][None], cv_ln_g[0][None], cv_ln_b[0][None],
                                  hy_short_w[0], hy_short_b[0][None])
    y_b = _hyena(gate1, gate2, v, hy_w1[0], hy_b1[0], hy_w2[0], hy_b2[0], hy_w3[0], hy_b3[0],
                 hy_w4[0], hy_freq[0], hy_decay[0], hy_skip[0])
    x2 = _proj_mlp(x2, [y_a.reshape(t, CONV_WIDTH), y_b.reshape(t, HYENA_WIDTH)],
                   ab_w_out[0].astype(BF16), norm_mlp[0][None], mlp_w_up[0].astype(BF16),
                   mlp_w_down[0].astype(BF16))

    q, k, v = _qkv(x2, norm_mix[1][None], at_w_qkv[0].astype(BF16), seq)
    o = _attention(q, k, v, at_sink[0], bsz, seq)
    x2 = _proj_mlp(x2, [o], at_w_o[0].astype(BF16), norm_mlp[1][None],
                   mlp_w_up[1].astype(BF16), mlp_w_down[1].astype(BF16), gf=norm_final[None])
    return x2.reshape(bsz, seq, D_MODEL)


def kernel(x_prompt, x_sample, norm_mix, norm_mlp, norm_final, ab_w_in, ab_w_out, cv_dw_w, cv_dw_b,
           cv_ln_g, cv_ln_b, hy_short_w, hy_short_b, hy_w1, hy_b1, hy_w2, hy_b2, hy_w3, hy_b3, hy_w4,
           hy_freq, hy_decay, hy_skip, at_w_qkv, at_sink, at_w_o, mlp_w_up, mlp_w_down):
    weights = (norm_mix, norm_mlp, norm_final, ab_w_in, ab_w_out, cv_dw_w, cv_dw_b, cv_ln_g,
               cv_ln_b, hy_short_w, hy_short_b, hy_w1, hy_b1, hy_w2, hy_b2, hy_w3, hy_b3, hy_w4,
               hy_freq, hy_decay, hy_skip, at_w_qkv, at_sink, at_w_o, mlp_w_up, mlp_w_down)
    return (_trunk(x_prompt, *weights), _trunk(x_sample, *weights))
```

```python
import functools
import math

import numpy as np
import jax
import jax.numpy as jnp
from jax import lax
from jax.experimental import pallas as pl
from jax.experimental.pallas import tpu as pltpu

F32 = jnp.float32
BF16 = jnp.bfloat16

D_MODEL = 1024
CONV_WIDTH = 512
CONV_KERNEL = 31
HYENA_WIDTH = 512
HYENA_IN = 3 * HYENA_WIDTH
HYENA_EMB_DIM = 33
FILTER_HIDDEN = 64
N_HEADS = 16
N_KV_HEADS = 4
HEAD_DIM = 64
ROT_DIM = 16
ROPE_THETA = 500000.0
WINDOW = 128
D_FF = 4 * D_MODEL
NORM_EPS = 1e-5
LN_EPS = 1e-5
FILTER_EPS = 1e-6

LANES = 128
DFT_P = 128
DFT_PH = DFT_P // 2
VMEM_LIMIT = 56 * 1024 * 1024


def _params(*sem):
    return pltpu.CompilerParams(dimension_semantics=sem, vmem_limit_bytes=VMEM_LIMIT)


def _resident(shape):
    nd = len(shape)
    return pl.BlockSpec(shape, lambda *_: (0,) * nd, pipeline_mode=pl.Buffered(1))


def _rms(x, g):
    return x * lax.rsqrt(jnp.mean(x * x, axis=-1, keepdims=True) + NORM_EPS) * g


CC_TR = 512
CC_HALO = 16
CC_EXT = CC_TR + 2 * CC_HALO
CC_RC = 32
SUBLANES = 8
CC_SHIFT0 = CC_HALO - CONV_KERNEL // 2
CC_MAX_ALIGNED = (CC_SHIFT0 + CONV_KERNEL - 1) // SUBLANES * SUBLANES
CC_SH_ROWS = CC_TR + CC_MAX_ALIGNED


def _front_kernel(*refs):
    _front_body(pl.program_id(1), pl.num_programs(1) - 1, *refs)


def _front_body(i, last, main_ref, prev_ref, next_ref, gn_ref, win_ref, w_ref, b_ref, g_ref,
                beta_ref, sw_ref, sb_ref, o_ref, x1_ref, x2_ref, v_ref, hn_buf, hbuf, hsh, ubuf):
    gn = gn_ref[...]
    hn_buf[0:CC_HALO, :] = _rms(prev_ref[...], gn).astype(BF16)
    hn_buf[CC_HALO:CC_HALO + CC_TR, :] = _rms(main_ref[...], gn).astype(BF16)
    hn_buf[CC_HALO + CC_TR:, :] = _rms(next_ref[...], gn).astype(BF16)
    first_row = jnp.where(i > 0, 0, CC_HALO)
    end_row = jnp.where(i < last, CC_EXT, CC_HALO + CC_TR)
    row = lax.broadcasted_iota(jnp.int32, (CC_EXT, 1), 0)
    valid = (row >= first_row) & (row < end_row)

    uc = jnp.dot(hn_buf[...], win_ref[:, :2 * CONV_WIDTH], preferred_element_type=F32)
    hbuf[...] = jnp.where(valid, uc[:, :CONV_WIDTH] * jax.nn.sigmoid(uc[:, CONV_WIDTH:]), 0.0)
    uh = jnp.dot(hn_buf[...], win_ref[:, 2 * CONV_WIDTH:], preferred_element_type=F32)
    ubuf[...] = jnp.where(valid, uh, 0.0)

    for n, s_ref in enumerate((x1_ref, x2_ref, v_ref)):
        cs = slice(n * HYENA_WIDTH, (n + 1) * HYENA_WIDTH)
        s_ref[...] = (ubuf[CC_HALO - 1:CC_HALO - 1 + CC_TR, cs] * sw_ref[0:1, cs]
                      + ubuf[CC_HALO:CC_HALO + CC_TR, cs] * sw_ref[1:2, cs]
                      + ubuf[CC_HALO + 1:CC_HALO + 1 + CC_TR, cs] * sw_ref[2:3, cs]
                      + sb_ref[:, cs]).astype(s_ref.dtype)

    for ph in range(SUBLANES):
        hsh[ph] = hbuf[ph:ph + CC_SH_ROWS, :]
    for r in range(CC_TR // CC_RC):
        acc = jnp.broadcast_to(b_ref[...], (CC_RC, CONV_WIDTH))
        for j in range(CONV_KERNEL):
            ph, lo = (j + CC_SHIFT0) % SUBLANES, r * CC_RC + (j + CC_SHIFT0) // SUBLANES * SUBLANES
            tap = jnp.tile(w_ref[j], (CC_RC // SUBLANES, 1))
            acc = acc + hsh[ph, lo:lo + CC_RC, :] * tap
        mu = jnp.mean(acc, axis=-1, keepdims=True)
        d = acc - mu
        var = jnp.mean(d * d, axis=-1, keepdims=True)
        y = d * lax.rsqrt(var + LN_EPS) * g_ref[...] + beta_ref[...]
        o_ref[r * CC_RC:(r + 1) * CC_RC, :] = (y * jax.nn.sigmoid(y)).astype(o_ref.dtype)


def _front(x, gn, w_in, cv_w, cv_b, ln_g, ln_b, short_w, short_b):
    bsz, seq, _ = x.shape
    nh = CC_TR // CC_HALO
    n_halo = seq // CC_HALO
    tile = lambda w: pl.BlockSpec((None, CC_TR, w), lambda bi, i: (bi, i, 0))
    hy_out = jax.ShapeDtypeStruct((bsz, seq, HYENA_WIDTH), BF16)
    return pl.pallas_call(
        _front_kernel,
        grid=(bsz, seq // CC_TR),
        in_specs=[
            tile(D_MODEL),
            pl.BlockSpec((None, CC_HALO, D_MODEL),
                         lambda bi, i: (bi, jnp.maximum(i * nh - 1, 0), 0)),
            pl.BlockSpec((None, CC_HALO, D_MODEL),
                         lambda bi, i: (bi, jnp.minimum((i + 1) * nh, n_halo - 1), 0)),
            _resident((1, D_MODEL)), _resident(w_in.shape),
            _resident((CONV_KERNEL, SUBLANES, CONV_WIDTH)), _resident((1, CONV_WIDTH)),
            _resident((1, CONV_WIDTH)), _resident((1, CONV_WIDTH)),
            _resident((3, HYENA_IN)), _resident((1, HYENA_IN)),
        ],
        out_specs=[tile(CONV_WIDTH), tile(HYENA_WIDTH), tile(HYENA_WIDTH), tile(HYENA_WIDTH)],
        out_shape=[jax.ShapeDtypeStruct((bsz, seq, CONV_WIDTH), BF16), hy_out, hy_out, hy_out],
        scratch_shapes=[pltpu.VMEM((CC_EXT, D_MODEL), BF16),
                        pltpu.VMEM((CC_EXT, CONV_WIDTH), F32),
                        pltpu.VMEM((SUBLANES, CC_SH_ROWS, CONV_WIDTH), F32),
                        pltpu.VMEM((CC_EXT, HYENA_IN), F32)],
        compiler_params=_params("parallel", "parallel"),
        name="front",
    )(x, x, x, gn, w_in,
      jnp.broadcast_to(cv_w[:, None, :], (CONV_KERNEL, SUBLANES, CONV_WIDTH)),
      cv_b, ln_g, ln_b, short_w, short_b)


FM_TL = 512
FEAT = 64
MASK_COL = HYENA_EMB_DIM
N_FILT = 2 * HYENA_WIDTH


@functools.lru_cache(maxsize=None)
def _filter_features(seq):
    t = np.linspace(0.0, 1.0, seq, dtype=np.float32).astype(np.float64)
    bands = (HYENA_EMB_DIM - 1) // 2
    w = (2.0 * math.pi * np.arange(seq, dtype=np.float32) / np.float32(seq)).astype(np.float64)
    f = np.linspace(1e-4, bands - 1, bands, dtype=np.float32).astype(np.float64)
    fw = w[:, None] * f[None, :]
    z = np.concatenate([t[:, None], np.cos(fw), -np.sin(fw)], axis=-1)
    fwd = np.zeros((seq, FEAT), np.float64)
    fwd[:, :HYENA_EMB_DIM] = z
    fwd[:, MASK_COL] = 1.0
    bwd = np.zeros((seq, FEAT), np.float64)
    bwd[1:, :HYENA_EMB_DIM] = z[:0:-1]
    bwd[1:, MASK_COL] = 1.0
    return np.concatenate([fwd, bwd], axis=-1).astype(np.float32)


def _filter_mlp_kernel(z_ref, w1_ref, b1_ref, w2_ref, b2_ref, w3_ref, b3_ref, w4_ref, fr_ref,
                       dec_ref, k_ref, ssq_ref):
    hp = lax.Precision.HIGHEST
    z = z_ref[...]
    h = jnp.sin(fr_ref[0:1, :] * (jnp.dot(z, w1_ref[...], precision=hp,
                                          preferred_element_type=F32) + b1_ref[...]))
    h = jnp.sin(fr_ref[1:2, :] * (jnp.dot(h, w2_ref[...], precision=hp,
                                          preferred_element_type=F32) + b2_ref[...]))
    h = jnp.sin(fr_ref[2:3, :] * (jnp.dot(h, w3_ref[...], precision=hp,
                                          preferred_element_type=F32) + b3_ref[...]))
    ssq = jnp.zeros((1, N_FILT), F32)
    hb = h.astype(BF16)
    for d in range(2):
        k = jnp.dot(hb, w4_ref[d], preferred_element_type=F32)
        t = z[:, d * FEAT:d * FEAT + 1]
        mask = z[:, d * FEAT + MASK_COL:d * FEAT + MASK_COL + 1]
        k = k * jnp.exp(-t * jnp.abs(dec_ref[d])) * mask
        k_ref[d] = k
        ssq = ssq + jnp.sum(k * k, axis=0, keepdims=True)

    @pl.when(pl.program_id(0) == 0)
    def _():
        ssq_ref[...] = jnp.zeros_like(ssq_ref)

    ssq_ref[...] += ssq


def _filter_mlp(seq, w1, b1, w2, b2, w3, b3, w4, freq, decay):
    fh = FILTER_HIDDEN

    def blockdiag(w):
        z = jnp.zeros_like(w)
        return jnp.concatenate([jnp.concatenate([w, z], 1), jnp.concatenate([z, w], 1)], 0)

    w1p = jnp.zeros((FEAT, fh), F32).at[:HYENA_EMB_DIM].set(w1)
    w4r = w4.reshape(fh, 2, 2, HYENA_WIDTH)
    zeros = jnp.zeros((fh, N_FILT), F32)
    w4f = jnp.concatenate([w4r[:, :, 0].reshape(fh, N_FILT), zeros], 0)
    w4b = jnp.concatenate([zeros, w4r[:, :, 1].reshape(fh, N_FILT)], 0)
    dec = decay.reshape(2, 2, HYENA_WIDTH)
    args = (
        jnp.asarray(_filter_features(seq)),
        blockdiag(w1p), jnp.tile(b1, 2)[None], blockdiag(w2), jnp.tile(b2, 2)[None],
        blockdiag(w3), jnp.tile(b3, 2)[None],
        jnp.stack([w4f, w4b]).astype(BF16),
        jnp.tile(freq, (1, 2)),
        jnp.stack([dec[:, 0].reshape(1, N_FILT), dec[:, 1].reshape(1, N_FILT)]),
    )
    in_specs = [pl.BlockSpec((FM_TL, 2 * FEAT), lambda i: (i, 0))]
    in_specs += [_resident(a.shape) for a in args[1:]]
    return pl.pallas_call(
        _filter_mlp_kernel,
        grid=(seq // FM_TL,),
        in_specs=in_specs,
        out_specs=[pl.BlockSpec((2, FM_TL, N_FILT), lambda i: (0, i, 0)),
                   pl.BlockSpec((1, N_FILT), lambda i: (0, 0))],
        out_shape=[jax.ShapeDtypeStruct((2, seq, N_FILT), F32),
                   jax.ShapeDtypeStruct((1, N_FILT), F32)],
        compiler_params=_params("arbitrary"),
        name="filter_mlp",
    )(*args)


@functools.lru_cache(maxsize=None)
def _dft_tables(q):
    n = DFT_P * q
    c = np.arange(DFT_P)[:, None]
    a = np.arange(DFT_P)[None, :]
    b = np.arange(q)[:, None, None]
    ang = -2.0 * np.pi * ((c * (q * a + b)) % n) / n
    fr, fi = np.cos(ang), np.sin(ang)
    d = np.arange(q)[:, None]
    ang2 = -2.0 * np.pi * ((d * np.arange(q)[None, :]) % q) / q
    gr, gi = np.cos(ang2), np.sin(ang2)
    g2 = np.block([[gr, -gi], [gi, gr]])
    g2i = np.block([[gr, gi], [-gi, gr]])
    return (fr.astype(np.float32), fi.astype(np.float32),
            g2.astype(np.float32), g2i.astype(np.float32))


def _dft_mats(q):
    fr, fi, g2, g2i = (jnp.asarray(t) for t in _dft_tables(q))
    frh, fih = fr[:, :, :DFT_PH], fi[:, :, :DFT_PH]
    m1 = jnp.concatenate([jnp.concatenate([frh, -fih], 2), jnp.concatenate([fih, frh], 2)], 1)
    m1f = jnp.concatenate([fr, fi], 1)
    hr, hi = jnp.swapaxes(frh, 1, 2), -jnp.swapaxes(fih, 1, 2)
    m3 = jnp.concatenate([jnp.concatenate([hr, -hi], 2), jnp.concatenate([hi, hr], 2)], 1)
    return (m1.astype(BF16), m1f.astype(BF16), m3.astype(BF16),
            g2.astype(BF16), g2i.astype(BF16))


S1_NB = 16
S1_CW = 256


def _dft_s1_kernel(x_ref, m_ref, ar_ref, ai_ref, re_buf, im_buf):
    x = x_ref[...].astype(BF16).reshape(DFT_P, S1_NB, S1_CW)
    xs = jnp.swapaxes(x, 0, 1)
    for j in range(S1_NB):
        r = jnp.dot(m_ref[j], xs[j], preferred_element_type=F32)
        re_buf[j] = r[:DFT_P].astype(BF16)
        im_buf[j] = r[DFT_P:].astype(BF16)
    ar_ref[...] = jnp.swapaxes(re_buf[...], 0, 1)
    ai_ref[...] = jnp.swapaxes(im_buf[...], 0, 1)


def _dft_stage1(x, m, q):
    g, wtot = x.shape[0] // 2, x.shape[3]
    out = jax.ShapeDtypeStruct((g, DFT_P, q, wtot), BF16)
    ospec = pl.BlockSpec((None, DFT_P, S1_NB, S1_CW), lambda gi, j, ci: (gi, 0, j, ci))
    stage = pltpu.VMEM((S1_NB, DFT_P, S1_CW), BF16)
    return pl.pallas_call(
        _dft_s1_kernel,
        grid=(g, q // S1_NB, wtot // S1_CW),
        in_specs=[pl.BlockSpec((2, DFT_PH, S1_NB, S1_CW), lambda gi, j, ci: (gi, 0, j, ci)),
                  pl.BlockSpec((S1_NB, 2 * DFT_P, DFT_P), lambda gi, j, ci: (j, 0, 0))],
        out_specs=[ospec, ospec],
        out_shape=[out, out],
        scratch_shapes=[stage, stage],
        compiler_params=_params("parallel", "parallel", "parallel"),
        name="dft_stage1",
    )(x, m)


S2_NC = 8
S2_RC = 16


def _filter_s2_kernel(ar_ref, ai_ref, g2_ref, ssq_ref, kr_ref, ki_ref, *, q, inv_n):
    scale = lax.rsqrt(ssq_ref[...] + FILTER_EPS) * inv_n
    for c in range(S2_NC):
        a = jnp.concatenate([ar_ref[c], ai_ref[c]], axis=0)
        x = jnp.dot(g2_ref[...], a, preferred_element_type=F32)
        kr_ref[c] = x[:q] * scale
        ki_ref[c] = x[q:] * scale


def _filter_stage2(ar, ai, g2, ssq, q):
    spec = pl.BlockSpec((S2_NC, q, N_FILT), lambda i: (i, 0, 0))
    out = jax.ShapeDtypeStruct((DFT_P, q, N_FILT), F32)
    return pl.pallas_call(
        functools.partial(_filter_s2_kernel, q=q, inv_n=1.0 / (DFT_P * q)),
        grid=(DFT_P // S2_NC,),
        in_specs=[spec, spec, _resident((2 * q, 2 * q)), _resident((1, N_FILT))],
        out_specs=[spec, spec],
        out_shape=[out, out],
        compiler_params=_params("parallel"),
        name="filter_stage2",
    )(ar, ai, g2, ssq)


def _dft_mid_kernel(ar_ref, ai_ref, g2_ref, g2i_ref, kr_ref, ki_ref, yr_ref, yi_ref,
                    x_scr, z_scr, *, q):
    for c in range(S2_NC):
        a = jnp.concatenate([ar_ref[c], ai_ref[c]], axis=0)
        x_scr[c] = jnp.dot(g2_ref[...], a, preferred_element_type=F32)
    for c in range(S2_NC):
        for r in range(q // S2_RC):
            re, im = slice(r * S2_RC, (r + 1) * S2_RC), slice(q + r * S2_RC, q + (r + 1) * S2_RC)
            xr, xi = x_scr[c, re, :], x_scr[c, im, :]
            kr, ki = kr_ref[c, re, :], ki_ref[c, re, :]
            z_scr[c, re, :] = (xr * kr - xi * ki).astype(BF16)
            z_scr[c, im, :] = (xr * ki + xi * kr).astype(BF16)
    for c in range(S2_NC):
        y = jnp.dot(g2i_ref[...], z_scr[c], preferred_element_type=F32)
        yr_ref[c] = y[:q].astype(BF16)
        yi_ref[c] = y[q:].astype(BF16)


def _dft_mid(ar, ai, g2, g2i, kr, ki, order, q):
    pairs = ar.shape[0]
    cw = HYENA_WIDTH
    spec = pl.BlockSpec((None, S2_NC, q, cw), lambda i, p: (p, i, 0, 0))
    kspec = pl.BlockSpec((S2_NC, q, cw), lambda i, p: (i, 0, order))
    out = jax.ShapeDtypeStruct(ar.shape, BF16)
    return pl.pallas_call(
        functools.partial(_dft_mid_kernel, q=q),
        grid=(DFT_P // S2_NC, pairs),
        in_specs=[spec, spec, _resident((2 * q, 2 * q)), _resident((2 * q, 2 * q)), kspec, kspec],
        out_specs=[spec, spec],
        out_shape=[out, out],
        scratch_shapes=[pltpu.VMEM((S2_NC, 2 * q, cw), F32), pltpu.VMEM((S2_NC, 2 * q, cw), BF16)],
        compiler_params=_params("parallel", "parallel"),
        name="dft_mid",
    )(ar, ai, g2, g2i, kr, ki)


def _dft_s3_kernel(yr_ref, yi_ref, m_ref, gate_ref, z_ref, sk_ref, o_ref, y_buf):
    yr = jnp.swapaxes(yr_ref[...], 0, 1)
    yi = jnp.swapaxes(yi_ref[...], 0, 1)
    for j in range(S1_NB):
        yc = jnp.concatenate([yr[j], yi[j]], axis=0)
        y_buf[j] = jnp.dot(m_ref[j], yc, preferred_element_type=F32)
    y = jnp.swapaxes(y_buf[...], 0, 1).reshape(2, DFT_PH, S1_NB, S1_CW)
    z = z_ref[...].astype(F32)
    o_ref[...] = (gate_ref[...].astype(F32) * (y + z * sk_ref[...])).astype(o_ref.dtype)


def _dft_stage3(yr, yi, m3, gate, z, sk, q):
    pairs, cw = yr.shape[0], yr.shape[3]
    yspec = pl.BlockSpec((None, DFT_P, S1_NB, S1_CW), lambda p, j, ci: (p, 0, j, ci))
    tspec = pl.BlockSpec((2, DFT_PH, S1_NB, S1_CW), lambda p, j, ci: (p, 0, j, ci))
    return pl.pallas_call(
        _dft_s3_kernel,
        grid=(pairs, q // S1_NB, cw // S1_CW),
        in_specs=[yspec, yspec,
                  pl.BlockSpec((S1_NB, DFT_P, 2 * DFT_P), lambda p, j, ci: (j, 0, 0)),
                  tspec, tspec, pl.BlockSpec((1, S1_CW), lambda p, j, ci: (0, ci))],
        out_specs=tspec,
        out_shape=jax.ShapeDtypeStruct(z.shape, BF16),
        scratch_shapes=[pltpu.VMEM((S1_NB, DFT_P, S1_CW), F32)],
        compiler_params=_params("parallel", "parallel", "parallel"),
        name="dft_stage3",
    )(yr, yi, m3, gate, z, sk)


def _hyena(x1, x2, v, w1, b1, w2, b2, w3, b3, w4, freq, decay, skip):
    bsz, seq, _ = v.shape
    q = 2 * seq // DFT_P
    cw = HYENA_WIDTH
    m1, m1f, m3, g2, g2i = _dft_mats(q)

    k_time, ssq = _filter_mlp(seq, w1, b1, w2, b2, w3, b3, w4, freq, decay)
    afr, afi = _dft_stage1(k_time.reshape(2, DFT_PH, q, N_FILT), m1f, q)
    kr, ki = _filter_stage2(afr.reshape(DFT_P, q, N_FILT), afi.reshape(DFT_P, q, N_FILT),
                            g2, ssq, q)

    time_view = (bsz, DFT_PH, q, cw)
    z = v.reshape(time_view)
    gates = (x1.reshape(time_view), x2.reshape(time_view))
    for n in range(2):
        ar, ai = _dft_stage1(z, m1, q)
        yr, yi = _dft_mid(ar, ai, g2, g2i, kr, ki, n, q)
        z = _dft_stage3(yr, yi, m3, gates[n], z, skip[n][None], q)
    return z.reshape(bsz, seq, cw)


PM_TM = 512
PM_FC = 1024


def _proj_mlp_kernel(*refs, n_parts, final):
    x_ref = refs[0]
    part_refs = refs[1:1 + n_parts]
    wo_ref, gm_ref, wup_ref, wdn_ref = refs[1 + n_parts:5 + n_parts]
    gf_ref = refs[5 + n_parts] if final else None
    o_ref = refs[-1]
    mixed = jnp.concatenate([p_ref[...] for p_ref in part_refs], axis=1)
    x = x_ref[...] + jnp.dot(mixed, wo_ref[...], preferred_element_type=F32)
    hn = _rms(x, gm_ref[...]).astype(BF16)
    for ch in range(D_FF // PM_FC):
        cs = slice(ch * PM_FC, (ch + 1) * PM_FC)
        h = jnp.dot(hn, wup_ref[:, cs], preferred_element_type=F32)
        h = jnp.square(jnp.maximum(h, 0.0)).astype(BF16)
        x = x + jnp.dot(h, wdn_ref[cs, :], preferred_element_type=F32)
    if final:
        x = _rms(x, gf_ref[...])
    o_ref[...] = x


def _proj_mlp_operands(x, parts, wo, gm, wup, wdn, gf):
    t = x.shape[0]
    row = lambda i: (i, 0)
    in_specs = [pl.BlockSpec((PM_TM, D_MODEL), row)]
    in_specs += [pl.BlockSpec((PM_TM, p.shape[1]), row) for p in parts]
    in_specs += [_resident(wo.shape), _resident((1, D_MODEL)), _resident(wup.shape),
                 _resident(wdn.shape)]
    args = [x, *parts, wo, gm, wup, wdn]
    if gf is not None:
        in_specs.append(_resident((1, D_MODEL)))
        args.append(gf)
    return dict(args=args, in_specs=in_specs,
                out_specs=pl.BlockSpec((PM_TM, D_MODEL), row),
                out_shape=jax.ShapeDtypeStruct((t, D_MODEL), F32))


def _proj_mlp(x, parts, wo, gm, wup, wdn, gf=None):
    ops = _proj_mlp_operands(x, parts, wo, gm, wup, wdn, gf)
    return pl.pallas_call(
        functools.partial(_proj_mlp_kernel, n_parts=len(parts), final=gf is not None),
        grid=(x.shape[0] // PM_TM,),
        in_specs=ops["in_specs"],
        out_specs=ops["out_specs"],
        out_shape=ops["out_shape"],
        compiler_params=_params("parallel"),
        name="proj_mlp",
    )(*ops["args"])


QK_TM = 512
Q_WIDTH = N_HEADS * HEAD_DIM
KV_WIDTH = N_KV_HEADS * HEAD_DIM
ROT_HALF = ROT_DIM // 2
LOG2E = math.log2(math.e)
Q_SCALE = HEAD_DIM ** -0.5 * LOG2E


@functools.lru_cache(maxsize=None)
def _rope_tables(seq):
    inv = ROPE_THETA ** (-(np.arange(0, ROT_DIM, 2, dtype=np.float64) / ROT_DIM))
    ang = np.arange(seq, dtype=np.float64)[:, None] * inv[None, :]
    cos = np.ones((seq, HEAD_DIM))
    s_lo = np.zeros((seq, HEAD_DIM))
    s_hi = np.zeros((seq, HEAD_DIM))
    cos[:, :ROT_HALF] = np.cos(ang)
    cos[:, ROT_HALF:ROT_DIM] = np.cos(ang)
    s_lo[:, ROT_HALF:ROT_DIM] = np.sin(ang)
    s_hi[:, :ROT_HALF] = -np.sin(ang)
    rep = LANES // HEAD_DIM
    return np.stack([np.tile(cos, (1, rep)), np.tile(s_lo, (1, rep)),
                     np.tile(s_hi, (1, rep))]).astype(np.float32)


def _qkv_kernel(x_ref, g_ref, w_ref, rope_ref, q_ref, k_ref, v_ref):
    hn = _rms(x_ref[...], g_ref[...]).astype(BF16)
    cos, s_lo, s_hi = rope_ref[0], rope_ref[1], rope_ref[2]
    nq = Q_WIDTH // LANES
    qk = jnp.dot(hn, w_ref[:, :Q_WIDTH + KV_WIDTH], preferred_element_type=F32)
    for blk in range((Q_WIDTH + KV_WIDTH) // LANES):
        t = qk[:, blk * LANES:(blk + 1) * LANES]
        r = (t * cos + pltpu.roll(t, ROT_HALF, axis=1) * s_lo
             + pltpu.roll(t, LANES - ROT_HALF, axis=1) * s_hi)
        if blk < nq:
            q_ref[:, blk * LANES:(blk + 1) * LANES] = (r * Q_SCALE).astype(BF16)
        else:
            k_ref[:, (blk - nq) * LANES:(blk - nq + 1) * LANES] = r.astype(BF16)
    v_ref[...] = jnp.dot(hn, w_ref[:, Q_WIDTH + KV_WIDTH:],
                         preferred_element_type=F32).astype(BF16)


def _qkv(x, g, w, seq):
    t = x.shape[0]
    per_seq = seq // QK_TM
    row = lambda i: (i, 0)
    return pl.pallas_call(
        _qkv_kernel,
        grid=(t // QK_TM,),
        in_specs=[pl.BlockSpec((QK_TM, D_MODEL), row), _resident((1, D_MODEL)),
                  _resident(w.shape),
                  pl.BlockSpec((3, QK_TM, LANES), lambda i: (0, i % per_seq, 0))],
        out_specs=[pl.BlockSpec((QK_TM, Q_WIDTH), row), pl.BlockSpec((QK_TM, KV_WIDTH), row),
                   pl.BlockSpec((QK_TM, KV_WIDTH), row)],
        out_shape=[jax.ShapeDtypeStruct((t, Q_WIDTH), BF16),
                   jax.ShapeDtypeStruct((t, KV_WIDTH), BF16),
                   jax.ShapeDtypeStruct((t, KV_WIDTH), BF16)],
        compiler_params=_params("parallel"),
        name="qkv_rope",
    )(x, g, w, jnp.asarray(_rope_tables(seq)))


AT_TQ = 512
AT_BLK = WINDOW
AT_KEYS = 3 * AT_BLK
AT_ROWS = AT_TQ + 2 * AT_BLK
AT_RC = 32
GROUP = N_HEADS // N_KV_HEADS


def _attn_kernel(*refs):
    _attn_body(pl.program_id(1), pl.num_programs(1) - 1, *refs)


def _attn_body(i, last, sink_ref, q_ref, k_ref, kp_ref, kn_ref, v_ref, vp_ref, vn_ref, o_ref,
               klo, khi, vlo, vhi, s_scr, p_scr, r_scr, unrolled=False):
    nb = AT_TQ // AT_BLK

    for (prev, main, nxt), lo_ref, hi_ref in (((kp_ref, k_ref, kn_ref), klo, khi),
                                              ((vp_ref, v_ref, vn_ref), vlo, vhi)):
        for row0, ref in ((0, prev), (AT_BLK, main), (AT_BLK + AT_TQ, nxt)):
            nrows = ref.shape[0]
            low = lax.broadcasted_iota(jnp.int32, (nrows, LANES), 1) < HEAD_DIM
            for kvh in range(N_KV_HEADS):
                t = ref[:, (kvh // 2) * LANES:(kvh // 2 + 1) * LANES]
                zero = jnp.zeros_like(t)
                if kvh % 2 == 0:
                    lo = jnp.where(low, t, zero)
                    hi = jnp.concatenate([zero[:, :HEAD_DIM], t[:, :HEAD_DIM]], axis=1)
                else:
                    hi = jnp.where(low, zero, t)
                    lo = jnp.concatenate([t[:, HEAD_DIM:], zero[:, :HEAD_DIM]], axis=1)
                lo_ref[kvh, row0:row0 + nrows, :] = lo
                hi_ref[kvh, row0:row0 + nrows, :] = hi

    rows2 = 2 * AT_BLK
    qrow = lax.broadcasted_iota(jnp.int32, (AT_RC, AT_BLK), 0)
    kcol = lax.broadcasted_iota(jnp.int32, (AT_RC, AT_BLK), 1)
    low_lane = lax.broadcasted_iota(jnp.int32, (AT_RC, LANES), 1) < HEAD_DIM

    def body(jb, carry):
        r0 = jb * AT_BLK if unrolled else pl.multiple_of(jb * AT_BLK, AT_BLK)
        off_prev = jnp.where((i == 0) & (jb == 0), AT_BLK, 0)
        off_next = jnp.where((i == last) & (jb == nb - 1), AT_BLK, 0)
        for kvh in range(N_KV_HEADS):
            col = kvh * GROUP * HEAD_DIM
            qs = jnp.concatenate([q_ref[pl.ds(r0, AT_BLK), col:col + LANES],
                                  q_ref[pl.ds(r0, AT_BLK), col + LANES:col + 2 * LANES]], axis=0)
            kc = jnp.concatenate([klo[kvh, pl.ds(r0, AT_KEYS), :],
                                  khi[kvh, pl.ds(r0, AT_KEYS), :]], axis=0)
            s_scr[kvh] = lax.dot_general(qs, kc, (((1,), (1,)), ((), ())),
                                         preferred_element_type=F32)
        for kvh in range(N_KV_HEADS):
            for rc in range(rows2 // AT_RC):
                rs = slice(rc * AT_RC, (rc + 1) * AT_RC)
                qr = qrow + (rc * AT_RC) % AT_BLK
                m_prev = kcol >= qr + off_prev
                m_next = kcol <= qr - off_next
                probs, rinv = [], []
                for half in range(2):
                    head = kvh * GROUP + half + (2 if rc * AT_RC >= AT_BLK else 0)
                    sink = sink_ref[head] * LOG2E
                    c0 = half * AT_KEYS
                    a = jnp.where(m_prev, s_scr[kvh, rs, c0:c0 + AT_BLK], -jnp.inf)
                    b = s_scr[kvh, rs, c0 + AT_BLK:c0 + 2 * AT_BLK]
                    c = jnp.where(m_next, s_scr[kvh, rs, c0 + 2 * AT_BLK:c0 + 3 * AT_BLK],
                                  -jnp.inf)
                    m = jnp.max(jnp.maximum(jnp.maximum(a, b), c), axis=-1, keepdims=True)
                    m = jnp.maximum(m, sink)
                    pa, pb, pc = jnp.exp2(a - m), jnp.exp2(b - m), jnp.exp2(c - m)
                    denom = jnp.sum(pa + pb + pc, axis=-1, keepdims=True) + jnp.exp2(sink - m)
                    probs += [pa, pb, pc]
                    rinv.append(1.0 / denom)
                p_scr[kvh, rs, :] = jnp.concatenate(probs, axis=1).astype(BF16)
                r_scr[kvh, rs, :] = jnp.where(low_lane, rinv[0], rinv[1])
        for kvh in range(N_KV_HEADS):
            col = kvh * GROUP * HEAD_DIM
            vc = jnp.concatenate([vlo[kvh, pl.ds(r0, AT_KEYS), :],
                                  vhi[kvh, pl.ds(r0, AT_KEYS), :]], axis=0)
            o = jnp.dot(p_scr[kvh], vc, preferred_element_type=F32)
            o = (o * r_scr[kvh]).astype(BF16)
            o_ref[pl.ds(r0, AT_BLK), col:col + LANES] = o[:AT_BLK]
            o_ref[pl.ds(r0, AT_BLK), col + LANES:col + 2 * LANES] = o[AT_BLK:]
        return carry

    if unrolled:
        for jb in range(nb):
            body(jb, 0)
    else:
        lax.fori_loop(0, nb, body, 0)


def _attention_operands(q, k, v, sink, bsz, seq, where):
    nb = AT_TQ // AT_BLK
    n_blk = seq // AT_BLK

    def spec(rows, width, pick):
        def index(*g):
            b, i = where(*g)
            return (b, pick(i), 0)
        return pl.BlockSpec((None, rows, width), index)

    main = lambda w: spec(AT_TQ, w, lambda i: i)
    prev = spec(AT_BLK, KV_WIDTH, lambda i: jnp.maximum(i * nb - 1, 0))
    nxt = spec(AT_BLK, KV_WIDTH, lambda i: jnp.minimum((i + 1) * nb, n_blk - 1))
    q3 = q.reshape(bsz, seq, Q_WIDTH)
    k3 = k.reshape(bsz, seq, KV_WIDTH)
    v3 = v.reshape(bsz, seq, KV_WIDTH)
    head_copy = pltpu.VMEM((N_KV_HEADS, AT_ROWS, LANES), BF16)
    return dict(
        args=(sink, q3, k3, k3, k3, v3, v3, v3),
        in_specs=[pl.BlockSpec(memory_space=pltpu.SMEM),
                  main(Q_WIDTH), main(KV_WIDTH), prev, nxt, main(KV_WIDTH), prev, nxt],
        out_specs=main(Q_WIDTH),
        out_shape=jax.ShapeDtypeStruct((bsz, seq, Q_WIDTH), BF16),
        scratch=[head_copy, head_copy, head_copy, head_copy,
                 pltpu.VMEM((N_KV_HEADS, 2 * AT_BLK, 2 * AT_KEYS), F32),
                 pltpu.VMEM((N_KV_HEADS, 2 * AT_BLK, 2 * AT_KEYS), BF16),
                 pltpu.VMEM((N_KV_HEADS, 2 * AT_BLK, LANES), F32)])


def _attention(q, k, v, sink, bsz, seq):
    ops = _attention_operands(q, k, v, sink, bsz, seq, lambda b, i: (b, i))
    out = pl.pallas_call(
        _attn_kernel,
        grid=(bsz, seq // AT_TQ),
        in_specs=ops["in_specs"],
        out_specs=ops["out_specs"],
        out_shape=ops["out_shape"],
        scratch_shapes=ops["scratch"],
        compiler_params=_params("parallel", "parallel"),
        name="window_attention",
    )(*ops["args"])
    return out.reshape(bsz * seq, Q_WIDTH)


def _proj_mlp_with_attention(mlp_inputs, attn_inputs):
    x, parts, wo, gm, wup, wdn, gf = mlp_inputs
    q, k, v, sink, bsz, seq = attn_inputs
    mlp = _proj_mlp_operands(x, parts, wo, gm, wup, wdn, gf)
    per_seq = seq // AT_TQ
    n_tiles = x.shape[0] // PM_TM
    assert bsz * per_seq == n_tiles
    att = _attention_operands(q, k, v, sink, bsz, seq, lambda t: (t // per_seq, t % per_seq))
    n_mlp, n_att = len(mlp["args"]), len(att["args"])

    def fused_kernel(*refs):
        mlp_in, att_in = refs[:n_mlp], refs[n_mlp:n_mlp + n_att]
        mlp_out, att_out = refs[n_mlp + n_att], refs[n_mlp + n_att + 1]
        scratch = refs[n_mlp + n_att + 2:]
        _proj_mlp_kernel(*mlp_in, mlp_out, n_parts=len(parts), final=gf is not None)
        _attn_body(pl.program_id(0) % per_seq, per_seq - 1, *att_in, att_out, *scratch,
                   unrolled=True)

    x_out, o = pl.pallas_call(
        fused_kernel,
        grid=(n_tiles,),
        in_specs=mlp["in_specs"] + att["in_specs"],
        out_specs=[mlp["out_specs"], att["out_specs"]],
        out_shape=[mlp["out_shape"], att["out_shape"]],
        scratch_shapes=att["scratch"],
        compiler_params=_params("parallel"),
        name="proj_mlp_attention",
    )(*mlp["args"], *att["args"])
    return x_out, o.reshape(bsz * seq, Q_WIDTH)


def kernel(x_prompt, x_sample, norm_mix, norm_mlp, norm_final, ab_w_in, ab_w_out, cv_dw_w, cv_dw_b,
           cv_ln_g, cv_ln_b, hy_short_w, hy_short_b, hy_w1, hy_b1, hy_w2, hy_b2, hy_w3, hy_b3, hy_w4,
           hy_freq, hy_decay, hy_skip, at_w_qkv, at_sink, at_w_o, mlp_w_up, mlp_w_down):
    w_in, w_out = ab_w_in[0].astype(BF16), ab_w_out[0].astype(BF16)
    w_qkv, w_o = at_w_qkv[0].astype(BF16), at_w_o[0].astype(BF16)
    w_up, w_down = mlp_w_up.astype(BF16), mlp_w_down.astype(BF16)

    def mixer0(x):
        bsz, seq, _ = x.shape
        y_a, gate1, gate2, v = _front(x, norm_mix[0][None], w_in, cv_dw_w[0], cv_dw_b[0][None],
                                      cv_ln_g[0][None], cv_ln_b[0][None], hy_short_w[0],
                                      hy_short_b[0][None])
        y_b = _hyena(gate1, gate2, v, hy_w1[0], hy_b1[0], hy_w2[0], hy_b2[0], hy_w3[0], hy_b3[0],
                     hy_w4[0], hy_freq[0], hy_decay[0], hy_skip[0])
        return [y_a.reshape(bsz * seq, CONV_WIDTH), y_b.reshape(bsz * seq, HYENA_WIDTH)]

    def mlp_inputs(layer, x2, parts):
        wo = w_out if layer == 0 else w_o
        gf = None if layer == 0 else norm_final[None]
        return (x2, parts, wo, norm_mlp[layer][None], w_up[layer], w_down[layer], gf)

    def attn_inputs(x2, x):
        bsz, seq, _ = x.shape
        q, k, v = _qkv(x2, norm_mix[1][None], w_qkv, seq)
        return (q, k, v, at_sink[0], bsz, seq)

    xa, xb = x_prompt, x_sample
    flat = lambda x: x.reshape(x.shape[0] * x.shape[1], D_MODEL)
    a1 = _proj_mlp(*mlp_inputs(0, flat(xa), mixer0(xa))[:6])
    att_a = attn_inputs(a1, xa)
    b1, o_a = _proj_mlp_with_attention(mlp_inputs(0, flat(xb), mixer0(xb)), att_a)
    att_b = attn_inputs(b1, xb)
    a2, o_b = _proj_mlp_with_attention(mlp_inputs(1, a1, [o_a]), att_b)
    b2 = _proj_mlp(*mlp_inputs(1, b1, [o_b]))
    return (a2.reshape(xa.shape), b2.reshape(xb.shape))
```

```python
import functools
import math

import numpy as np
import jax
import jax.numpy as jnp
from jax import lax
from jax.experimental import pallas as pl
from jax.experimental.pallas import tpu as pltpu

F32 = jnp.float32
BF16 = jnp.bfloat16

D_MODEL = 1024
CONV_WIDTH = 512
CONV_KERNEL = 31
HYENA_WIDTH = 512
HYENA_IN = 3 * HYENA_WIDTH
HYENA_EMB_DIM = 33
FILTER_HIDDEN = 64
N_HEADS = 16
N_KV_HEADS = 4
HEAD_DIM = 64
ROT_DIM = 16
ROPE_THETA = 500000.0
WINDOW = 128
D_FF = 4 * D_MODEL
NORM_EPS = 1e-5
LN_EPS = 1e-5
FILTER_EPS = 1e-6

LANES = 128
DFT_P = 128
DFT_PH = DFT_P // 2
VMEM_LIMIT = 56 * 1024 * 1024


def _params(*sem):
    return pltpu.CompilerParams(dimension_semantics=sem, vmem_limit_bytes=VMEM_LIMIT)


def _resident(shape):
    nd = len(shape)
    return pl.BlockSpec(shape, lambda *_: (0,) * nd, pipeline_mode=pl.Buffered(1))


def _rms(x, g):
    return x * lax.rsqrt(jnp.mean(x * x, axis=-1, keepdims=True) + NORM_EPS) * g


CC_TR = 512
CC_HALO = 16
CC_EXT = CC_TR + 2 * CC_HALO
CC_RC = 32
SUBLANES = 8
CC_SHIFT0 = CC_HALO - CONV_KERNEL // 2
CC_MAX_ALIGNED = (CC_SHIFT0 + CONV_KERNEL - 1) // SUBLANES * SUBLANES
CC_SH_ROWS = CC_TR + CC_MAX_ALIGNED


def _front_kernel(*refs):
    _front_body(pl.program_id(1), pl.num_programs(1) - 1, *refs)


def _front_body(i, last, main_ref, prev_ref, next_ref, gn_ref, win_ref, w_ref, b_ref, g_ref,
                beta_ref, sw_ref, sb_ref, o_ref, x1_ref, x2_ref, v_ref, hn_buf, hbuf, hsh, ubuf):
    gn = gn_ref[...]
    hn_buf[0:CC_HALO, :] = _rms(prev_ref[...], gn).astype(BF16)
    hn_buf[CC_HALO:CC_HALO + CC_TR, :] = _rms(main_ref[...], gn).astype(BF16)
    hn_buf[CC_HALO + CC_TR:, :] = _rms(next_ref[...], gn).astype(BF16)
    first_row = jnp.where(i > 0, 0, CC_HALO)
    end_row = jnp.where(i < last, CC_EXT, CC_HALO + CC_TR)
    row = lax.broadcasted_iota(jnp.int32, (CC_EXT, 1), 0)
    valid = (row >= first_row) & (row < end_row)

    uc = jnp.dot(hn_buf[...], win_ref[:, :2 * CONV_WIDTH], preferred_element_type=F32)
    hbuf[...] = jnp.where(valid, uc[:, :CONV_WIDTH] * jax.nn.sigmoid(uc[:, CONV_WIDTH:]), 0.0)
    uh = jnp.dot(hn_buf[...], win_ref[:, 2 * CONV_WIDTH:], preferred_element_type=F32)
    ubuf[...] = jnp.where(valid, uh, 0.0)

    for n, s_ref in enumerate((x1_ref, x2_ref, v_ref)):
        cs = slice(n * HYENA_WIDTH, (n + 1) * HYENA_WIDTH)
        s_ref[...] = (ubuf[CC_HALO - 1:CC_HALO - 1 + CC_TR, cs] * sw_ref[0:1, cs]
                      + ubuf[CC_HALO:CC_HALO + CC_TR, cs] * sw_ref[1:2, cs]
                      + ubuf[CC_HALO + 1:CC_HALO + 1 + CC_TR, cs] * sw_ref[2:3, cs]
                      + sb_ref[:, cs]).astype(s_ref.dtype)

    for ph in range(SUBLANES):
        hsh[ph] = hbuf[ph:ph + CC_SH_ROWS, :]
    for r in range(CC_TR // CC_RC):
        acc = jnp.broadcast_to(b_ref[...], (CC_RC, CONV_WIDTH))
        for j in range(CONV_KERNEL):
            ph, lo = (j + CC_SHIFT0) % SUBLANES, r * CC_RC + (j + CC_SHIFT0) // SUBLANES * SUBLANES
            tap = jnp.tile(w_ref[j], (CC_RC // SUBLANES, 1))
            acc = acc + hsh[ph, lo:lo + CC_RC, :] * tap
        mu = jnp.mean(acc, axis=-1, keepdims=True)
        d = acc - mu
        var = jnp.mean(d * d, axis=-1, keepdims=True)
        y = d * lax.rsqrt(var + LN_EPS) * g_ref[...] + beta_ref[...]
        o_ref[r * CC_RC:(r + 1) * CC_RC, :] = (y * jax.nn.sigmoid(y)).astype(o_ref.dtype)


def _front(x, gn, w_in, cv_w, cv_b, ln_g, ln_b, short_w, short_b):
    bsz, seq, _ = x.shape
    nh = CC_TR // CC_HALO
    n_halo = seq // CC_HALO
    tile = lambda w: pl.BlockSpec((None, CC_TR, w), lambda bi, i: (bi, i, 0))
    hy_out = jax.ShapeDtypeStruct((bsz, seq, HYENA_WIDTH), BF16)
    return pl.pallas_call(
        _front_kernel,
        grid=(bsz, seq // CC_TR),
        in_specs=[
            tile(D_MODEL),
            pl.BlockSpec((None, CC_HALO, D_MODEL),
                         lambda bi, i: (bi, jnp.maximum(i * nh - 1, 0), 0)),
            pl.BlockSpec((None, CC_HALO, D_MODEL),
                         lambda bi, i: (bi, jnp.minimum((i + 1) * nh, n_halo - 1), 0)),
            _resident((1, D_MODEL)), _resident(w_in.shape),
            _resident((CONV_KERNEL, SUBLANES, CONV_WIDTH)), _resident((1, CONV_WIDTH)),
            _resident((1, CONV_WIDTH)), _resident((1, CONV_WIDTH)),
            _resident((3, HYENA_IN)), _resident((1, HYENA_IN)),
        ],
        out_specs=[tile(CONV_WIDTH), tile(HYENA_WIDTH), tile(HYENA_WIDTH), tile(HYENA_WIDTH)],
        out_shape=[jax.ShapeDtypeStruct((bsz, seq, CONV_WIDTH), BF16), hy_out, hy_out, hy_out],
        scratch_shapes=[pltpu.VMEM((CC_EXT, D_MODEL), BF16),
                        pltpu.VMEM((CC_EXT, CONV_WIDTH), F32),
                        pltpu.VMEM((SUBLANES, CC_SH_ROWS, CONV_WIDTH), F32),
                        pltpu.VMEM((CC_EXT, HYENA_IN), F32)],
        compiler_params=_params("parallel", "parallel"),
        name="front",
    )(x, x, x, gn, w_in,
      jnp.broadcast_to(cv_w[:, None, :], (CONV_KERNEL, SUBLANES, CONV_WIDTH)),
      cv_b, ln_g, ln_b, short_w, short_b)


FM_TL = 512
FEAT = 64
MASK_COL = HYENA_EMB_DIM
N_FILT = 2 * HYENA_WIDTH


@functools.lru_cache(maxsize=None)
def _filter_features(seq):
    t = np.linspace(0.0, 1.0, seq, dtype=np.float32).astype(np.float64)
    bands = (HYENA_EMB_DIM - 1) // 2
    w = (2.0 * math.pi * np.arange(seq, dtype=np.float32) / np.float32(seq)).astype(np.float64)
    f = np.linspace(1e-4, bands - 1, bands, dtype=np.float32).astype(np.float64)
    fw = w[:, None] * f[None, :]
    z = np.concatenate([t[:, None], np.cos(fw), -np.sin(fw)], axis=-1)
    fwd = np.zeros((seq, FEAT), np.float64)
    fwd[:, :HYENA_EMB_DIM] = z
    fwd[:, MASK_COL] = 1.0
    bwd = np.zeros((seq, FEAT), np.float64)
    bwd[1:, :HYENA_EMB_DIM] = z[:0:-1]
    bwd[1:, MASK_COL] = 1.0
    return np.concatenate([fwd, bwd], axis=-1).astype(np.float32)


def _filter_mlp_kernel(z_ref, w1_ref, b1_ref, w2_ref, b2_ref, w3_ref, b3_ref, w4_ref, fr_ref,
                       dec_ref, k_ref, ssq_ref):
    hp = lax.Precision.HIGHEST
    z = z_ref[...]
    h = jnp.sin(fr_ref[0:1, :] * (jnp.dot(z, w1_ref[...], precision=hp,
                                          preferred_element_type=F32) + b1_ref[...]))
    h = jnp.sin(fr_ref[1:2, :] * (jnp.dot(h, w2_ref[...], precision=hp,
                                          preferred_element_type=F32) + b2_ref[...]))
    h = jnp.sin(fr_ref[2:3, :] * (jnp.dot(h, w3_ref[...], precision=hp,
                                          preferred_element_type=F32) + b3_ref[...]))
    ssq = jnp.zeros((1, N_FILT), F32)
    hb = h.astype(BF16)
    for d in range(2):
        k = jnp.dot(hb, w4_ref[d], preferred_element_type=F32)
        t = z[:, d * FEAT:d * FEAT + 1]
        mask = z[:, d * FEAT + MASK_COL:d * FEAT + MASK_COL + 1]
        k = k * jnp.exp(-t * jnp.abs(dec_ref[d])) * mask
        k_ref[d] = k
        ssq = ssq + jnp.sum(k * k, axis=0, keepdims=True)

    @pl.when(pl.program_id(0) == 0)
    def _():
        ssq_ref[...] = jnp.zeros_like(ssq_ref)

    ssq_ref[...] += ssq


def _filter_mlp(seq, w1, b1, w2, b2, w3, b3, w4, freq, decay):
    fh = FILTER_HIDDEN

    def blockdiag(w):
        z = jnp.zeros_like(w)
        return jnp.concatenate([jnp.concatenate([w, z], 1), jnp.concatenate([z, w], 1)], 0)

    w1p = jnp.zeros((FEAT, fh), F32).at[:HYENA_EMB_DIM].set(w1)
    w4r = w4.reshape(fh, 2, 2, HYENA_WIDTH)
    zeros = jnp.zeros((fh, N_FILT), F32)
    w4f = jnp.concatenate([w4r[:, :, 0].reshape(fh, N_FILT), zeros], 0)
    w4b = jnp.concatenate([zeros, w4r[:, :, 1].reshape(fh, N_FILT)], 0)
    dec = decay.reshape(2, 2, HYENA_WIDTH)
    args = (
        jnp.asarray(_filter_features(seq)),
        blockdiag(w1p), jnp.tile(b1, 2)[None], blockdiag(w2), jnp.tile(b2, 2)[None],
        blockdiag(w3), jnp.tile(b3, 2)[None],
        jnp.stack([w4f, w4b]).astype(BF16),
        jnp.tile(freq, (1, 2)),
        jnp.stack([dec[:, 0].reshape(1, N_FILT), dec[:, 1].reshape(1, N_FILT)]),
    )
    in_specs = [pl.BlockSpec((FM_TL, 2 * FEAT), lambda i: (i, 0))]
    in_specs += [_resident(a.shape) for a in args[1:]]
    return pl.pallas_call(
        _filter_mlp_kernel,
        grid=(seq // FM_TL,),
        in_specs=in_specs,
        out_specs=[pl.BlockSpec((2, FM_TL, N_FILT), lambda i: (0, i, 0)),
                   pl.BlockSpec((1, N_FILT), lambda i: (0, 0))],
        out_shape=[jax.ShapeDtypeStruct((2, seq, N_FILT), F32),
                   jax.ShapeDtypeStruct((1, N_FILT), F32)],
        compiler_params=_params("arbitrary"),
        name="filter_mlp",
    )(*args)


@functools.lru_cache(maxsize=None)
def _dft_tables(q):
    n = DFT_P * q
    c = np.arange(DFT_P)[:, None]
    a = np.arange(DFT_P)[None, :]
    b = np.arange(q)[:, None, None]
    ang = -2.0 * np.pi * ((c * (q * a + b)) % n) / n
    fr, fi = np.cos(ang), np.sin(ang)
    d = np.arange(q)[:, None]
    ang2 = -2.0 * np.pi * ((d * np.arange(q)[None, :]) % q) / q
    gr, gi = np.cos(ang2), np.sin(ang2)
    g2 = np.block([[gr, -gi], [gi, gr]])
    g2i = np.block([[gr, gi], [-gi, gr]])
    return (fr.astype(np.float32), fi.astype(np.float32),
            g2.astype(np.float32), g2i.astype(np.float32))


def _dft_mats(q):
    fr, fi, g2, g2i = (jnp.asarray(t) for t in _dft_tables(q))
    frh, fih = fr[:, :, :DFT_PH], fi[:, :, :DFT_PH]
    m1 = jnp.concatenate([jnp.concatenate([frh, -fih], 2), jnp.concatenate([fih, frh], 2)], 1)
    m1f = jnp.concatenate([fr, fi], 1)
    hr, hi = jnp.swapaxes(frh, 1, 2), -jnp.swapaxes(fih, 1, 2)
    m3 = jnp.concatenate([jnp.concatenate([hr, -hi], 2), jnp.concatenate([hi, hr], 2)], 1)
    return (m1.astype(BF16), m1f.astype(BF16), m3.astype(BF16),
            g2.astype(BF16), g2i.astype(BF16))


S1_NB = 16
S1_CW = 256


def _dft_s1_kernel(x_ref, m_ref, ar_ref, ai_ref, re_buf, im_buf):
    _dft_s1_compute(x_ref[...].astype(BF16), m_ref, ar_ref, ai_ref, re_buf, im_buf)


def _dft_s1_compute(x, m_ref, ar_ref, ai_ref, re_buf, im_buf):
    xs = jnp.swapaxes(x.reshape(DFT_P, S1_NB, S1_CW), 0, 1)
    for j in range(S1_NB):
        r = jnp.dot(m_ref[j], xs[j], preferred_element_type=F32)
        re_buf[j] = r[:DFT_P].astype(BF16)
        im_buf[j] = r[DFT_P:].astype(BF16)
    ar_ref[...] = jnp.swapaxes(re_buf[...], 0, 1)
    ai_ref[...] = jnp.swapaxes(im_buf[...], 0, 1)


def _dft_stage1(x, m, q):
    g, wtot = x.shape[0] // 2, x.shape[3]
    out = jax.ShapeDtypeStruct((g, DFT_P, q, wtot), BF16)
    ospec = pl.BlockSpec((None, DFT_P, S1_NB, S1_CW), lambda gi, j, ci: (gi, 0, j, ci))
    stage = pltpu.VMEM((S1_NB, DFT_P, S1_CW), BF16)
    return pl.pallas_call(
        _dft_s1_kernel,
        grid=(g, q // S1_NB, wtot // S1_CW),
        in_specs=[pl.BlockSpec((2, DFT_PH, S1_NB, S1_CW), lambda gi, j, ci: (gi, 0, j, ci)),
                  pl.BlockSpec((S1_NB, 2 * DFT_P, DFT_P), lambda gi, j, ci: (j, 0, 0))],
        out_specs=[ospec, ospec],
        out_shape=[out, out],
        scratch_shapes=[stage, stage],
        compiler_params=_params("parallel", "parallel", "parallel"),
        name="dft_stage1",
    )(x, m)


S2_NC = 8
S2_RC = 16


def _filter_s2_kernel(ar_ref, ai_ref, g2_ref, ssq_ref, kr_ref, ki_ref, *, q, inv_n):
    scale = lax.rsqrt(ssq_ref[...] + FILTER_EPS) * inv_n
    for c in range(S2_NC):
        a = jnp.concatenate([ar_ref[c], ai_ref[c]], axis=0)
        x = jnp.dot(g2_ref[...], a, preferred_element_type=F32)
        kr_ref[c] = x[:q] * scale
        ki_ref[c] = x[q:] * scale


def _filter_stage2(ar, ai, g2, ssq, q):
    spec = pl.BlockSpec((S2_NC, q, N_FILT), lambda i: (i, 0, 0))
    out = jax.ShapeDtypeStruct((DFT_P, q, N_FILT), F32)
    return pl.pallas_call(
        functools.partial(_filter_s2_kernel, q=q, inv_n=1.0 / (DFT_P * q)),
        grid=(DFT_P // S2_NC,),
        in_specs=[spec, spec, _resident((2 * q, 2 * q)), _resident((1, N_FILT))],
        out_specs=[spec, spec],
        out_shape=[out, out],
        compiler_params=_params("parallel"),
        name="filter_stage2",
    )(ar, ai, g2, ssq)


def _dft_mid_kernel(ar_ref, ai_ref, g2_ref, g2i_ref, kr_ref, ki_ref, yr_ref, yi_ref,
                    x_scr, z_scr, *, q):
    for c in range(S2_NC):
        a = jnp.concatenate([ar_ref[c], ai_ref[c]], axis=0)
        x_scr[c] = jnp.dot(g2_ref[...], a, preferred_element_type=F32)
    for c in range(S2_NC):
        for r in range(q // S2_RC):
            re, im = slice(r * S2_RC, (r + 1) * S2_RC), slice(q + r * S2_RC, q + (r + 1) * S2_RC)
            xr, xi = x_scr[c, re, :], x_scr[c, im, :]
            kr, ki = kr_ref[c, re, :], ki_ref[c, re, :]
            z_scr[c, re, :] = (xr * kr - xi * ki).astype(BF16)
            z_scr[c, im, :] = (xr * ki + xi * kr).astype(BF16)
    for c in range(S2_NC):
        y = jnp.dot(g2i_ref[...], z_scr[c], preferred_element_type=F32)
        yr_ref[c] = y[:q].astype(BF16)
        yi_ref[c] = y[q:].astype(BF16)


def _dft_mid(ar, ai, g2, g2i, kr, ki, order, q):
    pairs = ar.shape[0]
    cw = HYENA_WIDTH
    spec = pl.BlockSpec((None, S2_NC, q, cw), lambda i, p: (p, i, 0, 0))
    kspec = pl.BlockSpec((S2_NC, q, cw), lambda i, p: (i, 0, order))
    out = jax.ShapeDtypeStruct(ar.shape, BF16)
    return pl.pallas_call(
        functools.partial(_dft_mid_kernel, q=q),
        grid=(DFT_P // S2_NC, pairs),
        in_specs=[spec, spec, _resident((2 * q, 2 * q)), _resident((2 * q, 2 * q)), kspec, kspec],
        out_specs=[spec, spec],
        out_shape=[out, out],
        scratch_shapes=[pltpu.VMEM((S2_NC, 2 * q, cw), F32), pltpu.VMEM((S2_NC, 2 * q, cw), BF16)],
        compiler_params=_params("parallel", "parallel"),
        name="dft_mid",
    )(ar, ai, g2, g2i, kr, ki)


def _dft_s3_compute(yr_ref, yi_ref, m_ref, gate_ref, z_ref, sk_ref, y_buf):
    yr = jnp.swapaxes(yr_ref[...], 0, 1)
    yi = jnp.swapaxes(yi_ref[...], 0, 1)
    for j in range(S1_NB):
        yc = jnp.concatenate([yr[j], yi[j]], axis=0)
        y_buf[j] = jnp.dot(m_ref[j], yc, preferred_element_type=F32)
    y = jnp.swapaxes(y_buf[...], 0, 1).reshape(2, DFT_PH, S1_NB, S1_CW)
    z = z_ref[...].astype(F32)
    return (gate_ref[...].astype(F32) * (y + z * sk_ref[...])).astype(BF16)


def _dft_s3_kernel(yr_ref, yi_ref, m_ref, gate_ref, z_ref, sk_ref, o_ref, y_buf):
    o_ref[...] = _dft_s3_compute(yr_ref, yi_ref, m_ref, gate_ref, z_ref, sk_ref, y_buf)


def _dft_s3s1_kernel(yr_ref, yi_ref, m_ref, gate_ref, z_ref, sk_ref, m1_ref, o_ref, ar_ref, ai_ref,
                     y_buf, re_buf, im_buf):
    z_new = _dft_s3_compute(yr_ref, yi_ref, m_ref, gate_ref, z_ref, sk_ref, y_buf)
    o_ref[...] = z_new
    _dft_s1_compute(z_new, m1_ref, ar_ref, ai_ref, re_buf, im_buf)


def _dft_stage3(yr, yi, m3, gate, z, sk, q, m1=None):
    pairs, cw = yr.shape[0], yr.shape[3]
    index = lambda p, j, ci: (p, 0, j, ci)
    yspec = pl.BlockSpec((None, DFT_P, S1_NB, S1_CW), index)
    tspec = pl.BlockSpec((2, DFT_PH, S1_NB, S1_CW), index)
    mspec = lambda rows, cols: pl.BlockSpec((S1_NB, rows, cols), lambda p, j, ci: (j, 0, 0))
    in_specs = [yspec, yspec, mspec(DFT_P, 2 * DFT_P), tspec, tspec,
                pl.BlockSpec((1, S1_CW), lambda p, j, ci: (0, ci))]
    z_out = jax.ShapeDtypeStruct(z.shape, BF16)
    y_buf = pltpu.VMEM((S1_NB, DFT_P, S1_CW), F32)
    grid = (pairs, q // S1_NB, cw // S1_CW)
    params = _params("parallel", "parallel", "parallel")
    if m1 is None:
        return pl.pallas_call(
            _dft_s3_kernel, grid=grid, in_specs=in_specs, out_specs=tspec, out_shape=z_out,
            scratch_shapes=[y_buf], compiler_params=params, name="dft_stage3",
        )(yr, yi, m3, gate, z, sk)
    a_out = jax.ShapeDtypeStruct(yr.shape, BF16)
    stage = pltpu.VMEM((S1_NB, DFT_P, S1_CW), BF16)
    return pl.pallas_call(
        _dft_s3s1_kernel, grid=grid, in_specs=in_specs + [mspec(2 * DFT_P, DFT_P)],
        out_specs=[tspec, yspec, yspec], out_shape=[z_out, a_out, a_out],
        scratch_shapes=[y_buf, stage, stage], compiler_params=params, name="dft_stage3_stage1",
    )(yr, yi, m3, gate, z, sk, m1)


def _hyena(x1, x2, v, w1, b1, w2, b2, w3, b3, w4, freq, decay, skip):
    bsz, seq, _ = v.shape
    q = 2 * seq // DFT_P
    cw = HYENA_WIDTH
    m1, m1f, m3, g2, g2i = _dft_mats(q)

    k_time, ssq = _filter_mlp(seq, w1, b1, w2, b2, w3, b3, w4, freq, decay)
    afr, afi = _dft_stage1(k_time.reshape(2, DFT_PH, q, N_FILT), m1f, q)
    kr, ki = _filter_stage2(afr.reshape(DFT_P, q, N_FILT), afi.reshape(DFT_P, q, N_FILT),
                            g2, ssq, q)

    time_view = (bsz, DFT_PH, q, cw)
    z = v.reshape(time_view)
    gates = (x1.reshape(time_view), x2.reshape(time_view))
    ar, ai = _dft_stage1(z, m1, q)
    yr, yi = _dft_mid(ar, ai, g2, g2i, kr, ki, 0, q)
    z, ar, ai = _dft_stage3(yr, yi, m3, gates[0], z, skip[0][None], q, m1=m1)
    yr, yi = _dft_mid(ar, ai, g2, g2i, kr, ki, 1, q)
    z = _dft_stage3(yr, yi, m3, gates[1], z, skip[1][None], q)
    return z.reshape(bsz, seq, cw)


PM_TM = 512
PM_FC = 1024


def _proj_mlp_kernel(*refs, n_parts, final):
    x_ref = refs[0]
    part_refs = refs[1:1 + n_parts]
    wo_ref, gm_ref, wup_ref, wdn_ref = refs[1 + n_parts:5 + n_parts]
    gf_ref = refs[5 + n_parts] if final else None
    o_ref = refs[-1]
    mixed = jnp.concatenate([p_ref[...] for p_ref in part_refs], axis=1)
    x = x_ref[...] + jnp.dot(mixed, wo_ref[...], preferred_element_type=F32)
    hn = _rms(x, gm_ref[...]).astype(BF16)
    for ch in range(D_FF // PM_FC):
        cs = slice(ch * PM_FC, (ch + 1) * PM_FC)
        h = jnp.dot(hn, wup_ref[:, cs], preferred_element_type=F32)
        h = jnp.square(jnp.maximum(h, 0.0)).astype(BF16)
        x = x + jnp.dot(h, wdn_ref[cs, :], preferred_element_type=F32)
    if final:
        x = _rms(x, gf_ref[...])
    o_ref[...] = x


def _proj_mlp_operands(x, parts, wo, gm, wup, wdn, gf):
    t = x.shape[0]
    row = lambda i: (i, 0)
    in_specs = [pl.BlockSpec((PM_TM, D_MODEL), row)]
    in_specs += [pl.BlockSpec((PM_TM, p.shape[1]), row) for p in parts]
    in_specs += [_resident(wo.shape), _resident((1, D_MODEL)), _resident(wup.shape),
                 _resident(wdn.shape)]
    args = [x, *parts, wo, gm, wup, wdn]
    if gf is not None:
        in_specs.append(_resident((1, D_MODEL)))
        args.append(gf)
    return dict(args=args, in_specs=in_specs,
                out_specs=pl.BlockSpec((PM_TM, D_MODEL), row),
                out_shape=jax.ShapeDtypeStruct((t, D_MODEL), F32))


def _proj_mlp(x, parts, wo, gm, wup, wdn, gf=None):
    ops = _proj_mlp_operands(x, parts, wo, gm, wup, wdn, gf)
    return pl.pallas_call(
        functools.partial(_proj_mlp_kernel, n_parts=len(parts), final=gf is not None),
        grid=(x.shape[0] // PM_TM,),
        in_specs=ops["in_specs"],
        out_specs=ops["out_specs"],
        out_shape=ops["out_shape"],
        compiler_params=_params("parallel"),
        name="proj_mlp",
    )(*ops["args"])


QK_TM = 512
Q_WIDTH = N_HEADS * HEAD_DIM
KV_WIDTH = N_KV_HEADS * HEAD_DIM
ROT_HALF = ROT_DIM // 2
LOG2E = math.log2(math.e)
Q_SCALE = HEAD_DIM ** -0.5 * LOG2E


@functools.lru_cache(maxsize=None)
def _rope_tables(seq):
    inv = ROPE_THETA ** (-(np.arange(0, ROT_DIM, 2, dtype=np.float64) / ROT_DIM))
    ang = np.arange(seq, dtype=np.float64)[:, None] * inv[None, :]
    cos = np.ones((seq, HEAD_DIM))
    s_lo = np.zeros((seq, HEAD_DIM))
    s_hi = np.zeros((seq, HEAD_DIM))
    cos[:, :ROT_HALF] = np.cos(ang)
    cos[:, ROT_HALF:ROT_DIM] = np.cos(ang)
    s_lo[:, ROT_HALF:ROT_DIM] = np.sin(ang)
    s_hi[:, :ROT_HALF] = -np.sin(ang)
    rep = LANES // HEAD_DIM
    return np.stack([np.tile(cos, (1, rep)), np.tile(s_lo, (1, rep)),
                     np.tile(s_hi, (1, rep))]).astype(np.float32)


def _qkv_kernel(x_ref, g_ref, w_ref, rope_ref, q_ref, k_ref, v_ref):
    hn = _rms(x_ref[...], g_ref[...]).astype(BF16)
    cos, s_lo, s_hi = rope_ref[0], rope_ref[1], rope_ref[2]
    nq = Q_WIDTH // LANES
    qk = jnp.dot(hn, w_ref[:, :Q_WIDTH + KV_WIDTH], preferred_element_type=F32)
    for blk in range((Q_WIDTH + KV_WIDTH) // LANES):
        t = qk[:, blk * LANES:(blk + 1) * LANES]
        r = (t * cos + pltpu.roll(t, ROT_HALF, axis=1) * s_lo
             + pltpu.roll(t, LANES - ROT_HALF, axis=1) * s_hi)
        if blk < nq:
            q_ref[:, blk * LANES:(blk + 1) * LANES] = (r * Q_SCALE).astype(BF16)
        else:
            k_ref[:, (blk - nq) * LANES:(blk - nq + 1) * LANES] = r.astype(BF16)
    v_ref[...] = jnp.dot(hn, w_ref[:, Q_WIDTH + KV_WIDTH:],
                         preferred_element_type=F32).astype(BF16)


def _qkv(x, g, w, seq):
    t = x.shape[0]
    per_seq = seq // QK_TM
    row = lambda i: (i, 0)
    return pl.pallas_call(
        _qkv_kernel,
        grid=(t // QK_TM,),
        in_specs=[pl.BlockSpec((QK_TM, D_MODEL), row), _resident((1, D_MODEL)),
                  _resident(w.shape),
                  pl.BlockSpec((3, QK_TM, LANES), lambda i: (0, i % per_seq, 0))],
        out_specs=[pl.BlockSpec((QK_TM, Q_WIDTH), row), pl.BlockSpec((QK_TM, KV_WIDTH), row),
                   pl.BlockSpec((QK_TM, KV_WIDTH), row)],
        out_shape=[jax.ShapeDtypeStruct((t, Q_WIDTH), BF16),
                   jax.ShapeDtypeStruct((t, KV_WIDTH), BF16),
                   jax.ShapeDtypeStruct((t, KV_WIDTH), BF16)],
        compiler_params=_params("parallel"),
        name="qkv_rope",
    )(x, g, w, jnp.asarray(_rope_tables(seq)))


AT_TQ = 512
AT_BLK = WINDOW
AT_KEYS = 3 * AT_BLK
AT_ROWS = AT_TQ + 2 * AT_BLK
AT_RC = 32
GROUP = N_HEADS // N_KV_HEADS


def _attn_kernel(*refs):
    _attn_body(pl.program_id(1), pl.num_programs(1) - 1, *refs)


def _attn_body(i, last, sink_ref, q_ref, k_ref, kp_ref, kn_ref, v_ref, vp_ref, vn_ref, o_ref,
               klo, khi, vlo, vhi, s_scr, p_scr, r_scr, unrolled=False):
    nb = AT_TQ // AT_BLK

    for (prev, main, nxt), lo_ref, hi_ref in (((kp_ref, k_ref, kn_ref), klo, khi),
                                              ((vp_ref, v_ref, vn_ref), vlo, vhi)):
        for row0, ref in ((0, prev), (AT_BLK, main), (AT_BLK + AT_TQ, nxt)):
            nrows = ref.shape[0]
            low = lax.broadcasted_iota(jnp.int32, (nrows, LANES), 1) < HEAD_DIM
            for kvh in range(N_KV_HEADS):
                t = ref[:, (kvh // 2) * LANES:(kvh // 2 + 1) * LANES]
                zero = jnp.zeros_like(t)
                if kvh % 2 == 0:
                    lo = jnp.where(low, t, zero)
                    hi = jnp.concatenate([zero[:, :HEAD_DIM], t[:, :HEAD_DIM]], axis=1)
                else:
                    hi = jnp.where(low, zero, t)
                    lo = jnp.concatenate([t[:, HEAD_DIM:], zero[:, :HEAD_DIM]], axis=1)
                lo_ref[kvh, row0:row0 + nrows, :] = lo
                hi_ref[kvh, row0:row0 + nrows, :] = hi

    rows2 = 2 * AT_BLK
    qrow = lax.broadcasted_iota(jnp.int32, (AT_RC, AT_BLK), 0)
    kcol = lax.broadcasted_iota(jnp.int32, (AT_RC, AT_BLK), 1)
    low_lane = lax.broadcasted_iota(jnp.int32, (AT_RC, LANES), 1) < HEAD_DIM

    def body(jb, carry):
        r0 = jb * AT_BLK if unrolled else pl.multiple_of(jb * AT_BLK, AT_BLK)
        off_prev = jnp.where((i == 0) & (jb == 0), AT_BLK, 0)
        off_next = jnp.where((i == last) & (jb == nb - 1), AT_BLK, 0)
        for kvh in range(N_KV_HEADS):
            col = kvh * GROUP * HEAD_DIM
            qs = jnp.concatenate([q_ref[pl.ds(r0, AT_BLK), col:col + LANES],
                                  q_ref[pl.ds(r0, AT_BLK), col + LANES:col + 2 * LANES]], axis=0)
            kc = jnp.concatenate([klo[kvh, pl.ds(r0, AT_KEYS), :],
                                  khi[kvh, pl.ds(r0, AT_KEYS), :]], axis=0)
            s_scr[kvh] = lax.dot_general(qs, kc, (((1,), (1,)), ((), ())),
                                         preferred_element_type=F32)
        for kvh in range(N_KV_HEADS):
            for rc in range(rows2 // AT_RC):
                rs = slice(rc * AT_RC, (rc + 1) * AT_RC)
                qr = qrow + (rc * AT_RC) % AT_BLK
                m_prev = kcol >= qr + off_prev
                m_next = kcol <= qr - off_next
                probs, rinv = [], []
                for half in range(2):
                    head = kvh * GROUP + half + (2 if rc * AT_RC >= AT_BLK else 0)
                    sink = sink_ref[head] * LOG2E
                    c0 = half * AT_KEYS
                    a = jnp.where(m_prev, s_scr[kvh, rs, c0:c0 + AT_BLK], -jnp.inf)
                    b = s_scr[kvh, rs, c0 + AT_BLK:c0 + 2 * AT_BLK]
                    c = jnp.where(m_next, s_scr[kvh, rs, c0 + 2 * AT_BLK:c0 + 3 * AT_BLK],
                                  -jnp.inf)
                    m = jnp.max(jnp.maximum(jnp.maximum(a, b), c), axis=-1, keepdims=True)
                    m = jnp.maximum(m, sink)
                    pa, pb, pc = jnp.exp2(a - m), jnp.exp2(b - m), jnp.exp2(c - m)
                    denom = jnp.sum(pa + pb + pc, axis=-1, keepdims=True) + jnp.exp2(sink - m)
                    probs += [pa, pb, pc]
                    rinv.append(1.0 / denom)
                p_scr[kvh, rs, :] = jnp.concatenate(probs, axis=1).astype(BF16)
                r_scr[kvh, rs, :] = jnp.where(low_lane, rinv[0], rinv[1])
        for kvh in range(N_KV_HEADS):
            col = kvh * GROUP * HEAD_DIM
            vc = jnp.concatenate([vlo[kvh, pl.ds(r0, AT_KEYS), :],
                                  vhi[kvh, pl.ds(r0, AT_KEYS), :]], axis=0)
            o = jnp.dot(p_scr[kvh], vc, preferred_element_type=F32)
            o = (o * r_scr[kvh]).astype(BF16)
            o_ref[pl.ds(r0, AT_BLK), col:col + LANES] = o[:AT_BLK]
            o_ref[pl.ds(r0, AT_BLK), col + LANES:col + 2 * LANES] = o[AT_BLK:]
        return carry

    if unrolled:
        for jb in range(nb):
            body(jb, 0)
    else:
        lax.fori_loop(0, nb, body, 0)


def _attention_operands(q, k, v, sink, bsz, seq, where):
    nb = AT_TQ // AT_BLK
    n_blk = seq // AT_BLK

    def spec(rows, width, pick):
        def index(*g):
            b, i = where(*g)
            return (b, pick(i), 0)
        return pl.BlockSpec((None, rows, width), index)

    main = lambda w: spec(AT_TQ, w, lambda i: i)
    prev = spec(AT_BLK, KV_WIDTH, lambda i: jnp.maximum(i * nb - 1, 0))
    nxt = spec(AT_BLK, KV_WIDTH, lambda i: jnp.minimum((i + 1) * nb, n_blk - 1))
    q3 = q.reshape(bsz, seq, Q_WIDTH)
    k3 = k.reshape(bsz, seq, KV_WIDTH)
    v3 = v.reshape(bsz, seq, KV_WIDTH)
    head_copy = pltpu.VMEM((N_KV_HEADS, AT_ROWS, LANES), BF16)
    return dict(
        args=(sink, q3, k3, k3, k3, v3, v3, v3),
        in_specs=[pl.BlockSpec(memory_space=pltpu.SMEM),
                  main(Q_WIDTH), main(KV_WIDTH), prev, nxt, main(KV_WIDTH), prev, nxt],
        out_specs=main(Q_WIDTH),
        out_shape=jax.ShapeDtypeStruct((bsz, seq, Q_WIDTH), BF16),
        scratch=[head_copy, head_copy, head_copy, head_copy,
                 pltpu.VMEM((N_KV_HEADS, 2 * AT_BLK, 2 * AT_KEYS), F32),
                 pltpu.VMEM((N_KV_HEADS, 2 * AT_BLK, 2 * AT_KEYS), BF16),
                 pltpu.VMEM((N_KV_HEADS, 2 * AT_BLK, LANES), F32)])


def _attention(q, k, v, sink, bsz, seq):
    ops = _attention_operands(q, k, v, sink, bsz, seq, lambda b, i: (b, i))
    out = pl.pallas_call(
        _attn_kernel,
        grid=(bsz, seq // AT_TQ),
        in_specs=ops["in_specs"],
        out_specs=ops["out_specs"],
        out_shape=ops["out_shape"],
        scratch_shapes=ops["scratch"],
        compiler_params=_params("parallel", "parallel"),
        name="window_attention",
    )(*ops["args"])
    return out.reshape(bsz * seq, Q_WIDTH)


def _proj_mlp_with_attention(mlp_inputs, attn_inputs):
    x, parts, wo, gm, wup, wdn, gf = mlp_inputs
    q, k, v, sink, bsz, seq = attn_inputs
    mlp = _proj_mlp_operands(x, parts, wo, gm, wup, wdn, gf)
    per_seq = seq // AT_TQ
    n_tiles = x.shape[0] // PM_TM
    assert bsz * per_seq == n_tiles
    att = _attention_operands(q, k, v, sink, bsz, seq, lambda t: (t // per_seq, t % per_seq))
    n_mlp, n_att = len(mlp["args"]), len(att["args"])

    def fused_kernel(*refs):
        mlp_in, att_in = refs[:n_mlp], refs[n_mlp:n_mlp + n_att]
        mlp_out, att_out = refs[n_mlp + n_att], refs[n_mlp + n_att + 1]
        scratch = refs[n_mlp + n_att + 2:]
        _proj_mlp_kernel(*mlp_in, mlp_out, n_parts=len(parts), final=gf is not None)
        _attn_body(pl.program_id(0) % per_seq, per_seq - 1, *att_in, att_out, *scratch,
                   unrolled=True)

    x_out, o = pl.pallas_call(
        fused_kernel,
        grid=(n_tiles,),
        in_specs=mlp["in_specs"] + att["in_specs"],
        out_specs=[mlp["out_specs"], att["out_specs"]],
        out_shape=[mlp["out_shape"], att["out_shape"]],
        scratch_shapes=att["scratch"],
        compiler_params=_params("parallel"),
        name="proj_mlp_attention",
    )(*mlp["args"], *att["args"])
    return x_out, o.reshape(bsz * seq, Q_WIDTH)


def kernel(x_prompt, x_sample, norm_mix, norm_mlp, norm_final, ab_w_in, ab_w_out, cv_dw_w, cv_dw_b,
           cv_ln_g, cv_ln_b, hy_short_w, hy_short_b, hy_w1, hy_b1, hy_w2, hy_b2, hy_w3, hy_b3, hy_w4,
           hy_freq, hy_decay, hy_skip, at_w_qkv, at_sink, at_w_o, mlp_w_up, mlp_w_down):
    w_in, w_out = ab_w_in[0].astype(BF16), ab_w_out[0].astype(BF16)
    w_qkv, w_o = at_w_qkv[0].astype(BF16), at_w_o[0].astype(BF16)
    w_up, w_down = mlp_w_up.astype(BF16), mlp_w_down.astype(BF16)

    def mixer0(x):
        bsz, seq, _ = x.shape
        y_a, gate1, gate2, v = _front(x, norm_mix[0][None], w_in, cv_dw_w[0], cv_dw_b[0][None],
                                      cv_ln_g[0][None], cv_ln_b[0][None], hy_short_w[0],
                                      hy_short_b[0][None])
        y_b = _hyena(gate1, gate2, v, hy_w1[0], hy_b1[0], hy_w2[0], hy_b2[0], hy_w3[0], hy_b3[0],
                     hy_w4[0], hy_freq[0], hy_decay[0], hy_skip[0])
        return [y_a.reshape(bsz * seq, CONV_WIDTH), y_b.reshape(bsz * seq, HYENA_WIDTH)]

    def mlp_inputs(layer, x2, parts):
        wo = w_out if layer == 0 else w_o
        gf = None if layer == 0 else norm_final[None]
        return (x2, parts, wo, norm_mlp[layer][None], w_up[layer], w_down[layer], gf)

    def attn_inputs(x2, x):
        bsz, seq, _ = x.shape
        q, k, v = _qkv(x2, norm_mix[1][None], w_qkv, seq)
        return (q, k, v, at_sink[0], bsz, seq)

    xa, xb = x_prompt, x_sample
    flat = lambda x: x.reshape(x.shape[0] * x.shape[1], D_MODEL)
    a1 = _proj_mlp(*mlp_inputs(0, flat(xa), mixer0(xa))[:6])
    att_a = attn_inputs(a1, xa)
    b1, o_a = _proj_mlp_with_attention(mlp_inputs(0, flat(xb), mixer0(xb)), att_a)
    att_b = attn_inputs(b1, xb)
    a2, o_b = _proj_mlp_with_attention(mlp_inputs(1, a1, [o_a]), att_b)
    b2 = _proj_mlp(*mlp_inputs(1, b1, [o_b]))
    return (a2.reshape(xa.shape), b2.reshape(xb.shape))
```
